```python
import math
import jax, jax.numpy as jnp
from jax import lax
import numpy as np

D_MODEL = 4096
BATCH = 1
SEQ = 8192
DEPTH = 1
DEC_BATCH = 32
DEC_SEQ = 1
PAST_LEN = 8192
PAGE_SIZE = 128

HEAD_DIM = 128
MOBA_HEADS = D_MODEL // HEAD_DIM
MOBA_KV_HEADS = MOBA_HEADS // 4
MOBA_BLOCK = 256
MOBA_TOPK = 3
MOBA_Q_CHUNK = 16
DIFF_HEADS = D_MODEL // (2 * HEAD_DIM)
DIFF_KV_HEADS = DIFF_HEADS // 4
DIFF_Q_BLOCK = 128
ROPE_THETA = 10000.0
N_EXPERTS = 32
TOP_K = 4
D_FF = D_MODEL
SWIGLU_ALPHA = 1.702
SWIGLU_LIMIT = 7.0
NORM_EPS = 1e-6
SUBLN_EPS = 1e-5
NEG_INF = -1e30

kernel_name = 'hybrid_moba_diffattn_moe_decoder_step'


def _in_splits():
    return (MOBA_HEADS * HEAD_DIM, MOBA_KV_HEADS * HEAD_DIM, MOBA_KV_HEADS * HEAD_DIM,
            DIFF_HEADS * 2 * HEAD_DIM, DIFF_KV_HEADS * 2 * HEAD_DIM, DIFF_KV_HEADS * 2 * HEAD_DIM,
            D_MODEL, D_MODEL)


def _rmsnorm(x, g, eps=NORM_EPS):
    xf = x.astype(jnp.float32)
    y = xf * lax.rsqrt(jnp.mean(xf * xf, axis=-1, keepdims=True) + eps)
    return (y * g.astype(jnp.float32)).astype(x.dtype)


def _rope(x, pos):
    half = HEAD_DIM // 2
    inv_freq = 1.0 / (ROPE_THETA ** (jnp.arange(half, dtype=jnp.float32) / half))
    ang = pos.astype(jnp.float32)[:, None] * inv_freq[None, :]
    shape = (1, pos.shape[0]) + (1,) * (x.ndim - 3) + (half,)
    cos = jnp.cos(ang).reshape(shape)
    sin = jnp.sin(ang).reshape(shape)
    xf = x.astype(jnp.float32)
    x1, x2 = xf[..., :half], xf[..., half:]
    return jnp.concatenate([x1 * cos - x2 * sin, x2 * cos + x1 * sin], axis=-1).astype(x.dtype)


def _map_query_blocks(fn, q, pos, block):
    b, s = q.shape[0], q.shape[1]
    qb = min(block, s)
    n = -(-s // qb)
    pad = n * qb - s
    if pad:
        q = jnp.pad(q, [(0, 0), (0, pad)] + [(0, 0)] * (q.ndim - 2))
        pos = jnp.concatenate([pos, jnp.full((pad,), pos[-1], pos.dtype)])
    q_blocks = jnp.moveaxis(q.reshape((b, n, qb) + q.shape[2:]), 1, 0)
    out = lax.map(lambda a: fn(a[0], a[1]), (q_blocks, pos.reshape(n, qb)))
    out = jnp.moveaxis(out, 0, 1)
    return out.reshape((b, n * qb) + out.shape[3:])[:, :s]


def _moba(q, k, v, pos):
    bsz, L = q.shape[0], k.shape[1]
    nb = -(-L // MOBA_BLOCK)
    pad = nb * MOBA_BLOCK - L
    kp = jnp.pad(k, ((0, 0), (0, pad), (0, 0), (0, 0))).reshape(bsz, nb, MOBA_BLOCK, MOBA_KV_HEADS, HEAD_DIM)
    vp = jnp.pad(v, ((0, 0), (0, pad), (0, 0), (0, 0))).reshape(bsz, nb, MOBA_BLOCK, MOBA_KV_HEADS, HEAD_DIM)
    k_mean = jnp.mean(kp.astype(jnp.float32), axis=2)
    k_bt = jnp.moveaxis(kp, 3, 1)
    v_bt = jnp.moveaxis(vp, 3, 1)
    n_sel = min(MOBA_TOPK, nb - 1)
    grp = MOBA_HEADS // MOBA_KV_HEADS
    scale = HEAD_DIM ** -0.5
    b_idx = jnp.arange(bsz)[:, None, None, None, None]
    h_idx = jnp.arange(MOBA_KV_HEADS)[None, None, :, None, None]
    offs = jnp.arange(MOBA_BLOCK)

    def attend(q_c, pos_c):
        qc = q_c.shape[1]
        qf = q_c.reshape(bsz, qc, MOBA_KV_HEADS, grp, HEAD_DIM).astype(jnp.float32)
        own = pos_c // MOBA_BLOCK
        k_own = jnp.take(kp, own, axis=1).astype(jnp.float32)
        v_own = jnp.take(vp, own, axis=1).astype(jnp.float32)
        s_own = jnp.einsum('bqkgd,bqjkd->bqkgj', qf, k_own) * scale
        own_ok = (own[:, None] * MOBA_BLOCK + offs[None, :]) <= pos_c[:, None]
        s_own = jnp.where(own_ok[None, :, None, None, :], s_own, NEG_INF)
        if n_sel == 0:
            p_own = jax.nn.softmax(s_own, axis=-1)
            out = jnp.einsum('bqkgj,bqjkd->bqkgd', p_own, v_own)
        else:
            gate = jnp.einsum('bqkgd,bnkd->bqkgn', qf, k_mean)
            past_ok = jnp.arange(nb)[None, :] < own[:, None]
            gate = jnp.where(past_ok[None, :, None, None, :], gate, NEG_INF)
            top_val, top_idx = lax.top_k(gate, n_sel)
            sel_ok = top_val > 0.5 * NEG_INF
            k_sel = k_bt[b_idx, h_idx, top_idx].astype(jnp.float32)
            v_sel = v_bt[b_idx, h_idx, top_idx].astype(jnp.float32)
            s_sel = jnp.einsum('bqkgd,bqkgnjd->bqkgnj', qf, k_sel) * scale
            s_sel = jnp.where(sel_ok[..., None], s_sel, NEG_INF)
            s_all = jnp.concatenate([s_sel.reshape(bsz, qc, MOBA_KV_HEADS, grp, n_sel * MOBA_BLOCK), s_own], axis=-1)
            p = jax.nn.softmax(s_all, axis=-1)
            p_sel = p[..., :n_sel * MOBA_BLOCK].reshape(bsz, qc, MOBA_KV_HEADS, grp, n_sel, MOBA_BLOCK)
            p_own = p[..., n_sel * MOBA_BLOCK:]
            out = (jnp.einsum('bqkgnj,bqkgnjd->bqkgd', p_sel, v_sel)
                   + jnp.einsum('bqkgj,bqjkd->bqkgd', p_own, v_own))
        return out.reshape(bsz, qc, MOBA_HEADS * HEAD_DIM).astype(q.dtype)

    return _map_query_blocks(attend, q, pos, MOBA_Q_CHUNK)


def _diff_attn(q, k, v, pos, lam, subln_w, lam_init):
    bsz, L = q.shape[0], k.shape[1]
    grp = DIFF_HEADS // DIFF_KV_HEADS
    scale = HEAD_DIM ** -0.5
    kf = k.astype(jnp.float32)
    vf = v.astype(jnp.float32)
    key_pos = jnp.arange(L)

    def attend(q_c, pos_c):
        qc = q_c.shape[1]
        qf = q_c.reshape(bsz, qc, DIFF_KV_HEADS, grp, 2, HEAD_DIM).astype(jnp.float32)
        s = jnp.einsum('bqkgtd,bjktd->bkgtqj', qf, kf) * scale
        causal = key_pos[None, :] <= pos_c[:, None]
        s = jnp.where(causal[None, None, None, None], s, NEG_INF)
        p = jax.nn.softmax(s, axis=-1)
        a = p[:, :, :, 0] - lam * p[:, :, :, 1]
        o = jnp.einsum('bkgqj,bjkd->bqkgd', a, vf)
        o = _rmsnorm(o, subln_w, SUBLN_EPS) * (1.0 - lam_init)
        return o.reshape(bsz, qc, DIFF_HEADS * 2 * HEAD_DIM).astype(q.dtype)

    return _map_query_blocks(attend, q, pos, DIFF_Q_BLOCK)


def _swiglu_clamped(u):
    g = jnp.minimum(u[..., ::2], SWIGLU_LIMIT)
    lin = jnp.clip(u[..., 1::2], -SWIGLU_LIMIT, SWIGLU_LIMIT)
    return g * jax.nn.sigmoid(SWIGLU_ALPHA * g) * (lin + 1.0)


def _moe(h, w_router, b_router, w1, b1, w2, b2):
    bsz, s, d = h.shape
    hf = h.reshape(bsz * s, d)
    logits = (hf @ w_router).astype(jnp.float32) + b_router.astype(jnp.float32)
    top_val, top_idx = lax.top_k(logits, TOP_K)
    wts = jax.nn.softmax(top_val, axis=-1)
    combine = jnp.einsum('tk,tke->te', wts, jax.nn.one_hot(top_idx, N_EXPERTS, dtype=jnp.float32))
    out = jnp.zeros((bsz * s, d), jnp.float32)
    for e in range(N_EXPERTS):
        y = _swiglu_clamped(hf @ w1[e] + b1[e]) @ w2[e] + b2[e]
        out = out + combine[:, e:e + 1] * y.astype(jnp.float32)
    return out.reshape(bsz, s, d).astype(h.dtype)


def _block(x, c, past_kv, p, lam_init):
    (w_ada, b_ada, g_pre_mix, g_post_mix, g_pre_ffn, g_post_ffn, w_in, w_out,
     lam_q1, lam_k1, lam_q2, lam_k2, diff_subln, w_router, b_router,
     w_mlp1, b_mlp1, w_mlp2, b_mlp2) = p
    bsz, s, _ = x.shape
    past_len = 0 if past_kv is None else past_kv[0].shape[1]
    pos = jnp.arange(past_len, past_len + s, dtype=jnp.int32)
    mod = (jax.nn.silu(c) @ w_ada + b_ada).reshape(bsz, 6, D_MODEL)
    shift_m, scale_m, gate_m, shift_f, scale_f, gate_f = [mod[:, i, None, :] for i in range(6)]

    h = _rmsnorm(x, g_pre_mix) * (1.0 + scale_m) + shift_m
    proj = h @ w_in
    points = [int(t) for t in np.cumsum(_in_splits())[:-1]]
    qa, ka, va, qb, kb, vb, ga, gb = jnp.split(proj, points, axis=-1)
    qa = _rope(qa.reshape(bsz, s, MOBA_HEADS, HEAD_DIM), pos)
    ka = _rope(ka.reshape(bsz, s, MOBA_KV_HEADS, HEAD_DIM), pos)
    va = va.reshape(bsz, s, MOBA_KV_HEADS, HEAD_DIM)
    qb = _rope(qb.reshape(bsz, s, DIFF_HEADS, 2, HEAD_DIM), pos)
    kb = _rope(kb.reshape(bsz, s, DIFF_KV_HEADS, 2, HEAD_DIM), pos)
    vb = vb.reshape(bsz, s, DIFF_KV_HEADS, 2 * HEAD_DIM)
    new_rows = (ka, va, kb, vb)
    if past_kv is None:
        full = new_rows
    else:
        full = tuple(jnp.concatenate([old, new], axis=1) for old, new in zip(past_kv, new_rows))
    o_a = _moba(qa, full[0], full[1], pos)
    lam = (jnp.exp(jnp.sum(lam_q1.astype(jnp.float32) * lam_k1.astype(jnp.float32)))
           - jnp.exp(jnp.sum(lam_q2.astype(jnp.float32) * lam_k2.astype(jnp.float32))) + lam_init)
    o_b = _diff_attn(qb, full[2], full[3], pos, lam, diff_subln, lam_init)
    merged = jax.nn.sigmoid(ga) * o_a + jax.nn.sigmoid(gb) * o_b
    x = x + gate_m * _rmsnorm(merged @ w_out, g_post_mix)

    h2 = _rmsnorm(x, g_pre_ffn) * (1.0 + scale_f) + shift_f
    x = x + gate_f * _rmsnorm(_moe(h2, w_router, b_router, w_mlp1, b_mlp1, w_mlp2, b_mlp2), g_post_ffn)
    return x, new_rows


def setup_inputs(seed: int = 0) -> dict:
    key = jax.random.key(seed)
    ks = jax.random.split(key, 32)
    n_pages = PAST_LEN // PAGE_SIZE
    n_used = DEC_BATCH * n_pages
    n_pool = n_used + max(1, n_used // 4)
    page_table = jax.random.permutation(ks[0], n_pool)[:n_used].reshape(DEC_BATCH, n_pages).astype(jnp.int32)
    d_in = sum(_in_splits())

    def nrm(i, shape, scale):
        return jax.random.normal(ks[i], shape, jnp.float32) * scale

    return {
        'x_prompt': nrm(1, (BATCH, SEQ, D_MODEL), 1.0),
        'x_sample': nrm(2, (DEC_BATCH, DEC_SEQ, D_MODEL), 1.0),
        'c_prompt': nrm(3, (BATCH, D_MODEL), 1.0),
        'c_sample': nrm(4, (DEC_BATCH, D_MODEL), 1.0),
        'cache_moba_k': nrm(5, (DEPTH, n_pool, PAGE_SIZE, MOBA_KV_HEADS, HEAD_DIM), 1.0),
        'cache_moba_v': nrm(6, (DEPTH, n_pool, PAGE_SIZE, MOBA_KV_HEADS, HEAD_DIM), 1.0),
        'cache_diff_k': nrm(7, (DEPTH, n_pool, PAGE_SIZE, DIFF_KV_HEADS, 2, HEAD_DIM), 1.0),
        'cache_diff_v': nrm(8, (DEPTH, n_pool, PAGE_SIZE, DIFF_KV_HEADS, 2 * HEAD_DIM), 1.0),
        'page_table': page_table,
        'w_ada': nrm(9, (DEPTH, D_MODEL, 6 * D_MODEL), 0.5 * D_MODEL ** -0.5),
        'b_ada': nrm(10, (DEPTH, 6 * D_MODEL), 0.02),
        'g_pre_mix': 1.0 + nrm(11, (DEPTH, D_MODEL), 0.05),
        'g_post_mix': 1.0 + nrm(12, (DEPTH, D_MODEL), 0.05),
        'g_pre_ffn': 1.0 + nrm(13, (DEPTH, D_MODEL), 0.05),
        'g_post_ffn': 1.0 + nrm(14, (DEPTH, D_MODEL), 0.05),
        'w_in': nrm(15, (DEPTH, D_MODEL, d_in), D_MODEL ** -0.5),
        'w_out': nrm(16, (DEPTH, D_MODEL, D_MODEL), D_MODEL ** -0.5),
        'lambda_q1': nrm(17, (DEPTH, HEAD_DIM), 0.1),
        'lambda_k1': nrm(18, (DEPTH, HEAD_DIM), 0.1),
        'lambda_q2': nrm(19, (DEPTH, HEAD_DIM), 0.1),
        'lambda_k2': nrm(20, (DEPTH, HEAD_DIM), 0.1),
        'diff_subln': 1.0 + nrm(21, (DEPTH, 2 * HEAD_DIM), 0.05),
        'w_router': nrm(22, (DEPTH, D_MODEL, N_EXPERTS), D_MODEL ** -0.5),
        'b_router': nrm(23, (DEPTH, N_EXPERTS), 0.01),
        'w_mlp1': nrm(24, (DEPTH, N_EXPERTS, D_MODEL, 2 * D_FF), D_MODEL ** -0.5),
        'b_mlp1': nrm(25, (DEPTH, N_EXPERTS, 2 * D_FF), 0.01),
        'w_mlp2': nrm(26, (DEPTH, N_EXPERTS, D_FF, D_MODEL), D_FF ** -0.5),
        'b_mlp2': nrm(27, (DEPTH, N_EXPERTS, D_MODEL), 0.01),
    }


def reference(x_prompt, x_sample, c_prompt, c_sample, cache_moba_k, cache_moba_v, cache_diff_k, cache_diff_v,
              page_table, w_ada, b_ada, g_pre_mix, g_post_mix, g_pre_ffn, g_post_ffn, w_in, w_out,
              lambda_q1, lambda_k1, lambda_q2, lambda_k2, diff_subln, w_router, b_router,
              w_mlp1, b_mlp1, w_mlp2, b_mlp2):
    n_dec, n_pages = page_table.shape

    def gather_past(cache):
        rows = cache[page_table]
        return rows.reshape((n_dec, n_pages * cache.shape[1]) + cache.shape[2:])

    y_prompt, y_sample = x_prompt, x_sample
    rows_p = ([], [], [], [])
    rows_s = ([], [], [], [])
    for l in range(DEPTH):
        p = (w_ada[l], b_ada[l], g_pre_mix[l], g_post_mix[l], g_pre_ffn[l], g_post_ffn[l], w_in[l], w_out[l],
             lambda_q1[l], lambda_k1[l], lambda_q2[l], lambda_k2[l], diff_subln[l], w_router[l], b_router[l],
             w_mlp1[l], b_mlp1[l], w_mlp2[l], b_mlp2[l])
        lam_init = 0.8 - 0.6 * math.exp(-0.3 * l)
        y_prompt, new_p = _block(y_prompt, c_prompt, None, p, lam_init)
        past = (gather_past(cache_moba_k[l]), gather_past(cache_moba_v[l]),
                gather_past(cache_diff_k[l]), gather_past(cache_diff_v[l]))
        y_sample, new_s = _block(y_sample, c_sample, past, p, lam_init)
        for i in range(4):
            rows_p[i].append(new_p[i])
            rows_s[i].append(new_s[i])
    moba_k_prompt, moba_v_prompt, diff_k_prompt, diff_v_prompt = [jnp.stack(r, axis=0) for r in rows_p]
    moba_k_sample, moba_v_sample, diff_k_sample, diff_v_sample = [jnp.stack(r, axis=0) for r in rows_s]
    return (y_prompt, y_sample, moba_k_prompt, moba_v_prompt, diff_k_prompt, diff_v_prompt,
            moba_k_sample, moba_v_sample, diff_k_sample, diff_v_sample)
```

```python
import functools
import math

import jax
import jax.numpy as jnp
from jax import lax
from jax.experimental import pallas as pl
from jax.experimental.pallas import tpu as pltpu

F32 = jnp.float32
BF16 = jnp.bfloat16

HEAD_DIM = 128
MOBA_BLOCK = 256
MOBA_TOPK = 3
MOBA_GROUP = 4
DIFF_GROUP = 4
PAGE_SIZE = 128
ROPE_THETA = 10000.0
TOP_K = 4
SWIGLU_ALPHA = 1.702
SWIGLU_LIMIT = 7.0
NORM_EPS = 1e-6
SUBLN_EPS = 1e-5
NEG_INF = -1e30
LANES = 128
VMEM_LIMIT = 56 * 1024 * 1024

EXPERT_ROWS = 1280
EXPERT_SUB = 256
EXPERT_TN = 256
ROUTE_TM = 256
COMBINE_TM = 64
DMA_WINDOW = 32


def _cparams(sem, vmem=VMEM_LIMIT):
    return pltpu.CompilerParams(dimension_semantics=sem, vmem_limit_bytes=vmem)


def _rms(x, eps):
    return x * lax.rsqrt(jnp.mean(x * x, axis=-1, keepdims=True) + eps)


def _mm_kernel(a_ref, w_ref, *rest, rope, bias, silu_a, precise):
    o_ref = rest[-1] if precise else rest[-2]
    a = a_ref[...]
    if silu_a:
        a = a * jax.nn.sigmoid(a)
    if precise:
        acc = jnp.dot(a, w_ref[...], precision=lax.Precision.HIGHEST, preferred_element_type=F32)
    else:
        wb_ref = rest[-1]

        @pl.when(pl.program_id(1) == 0)
        def _():
            wb_ref[...] = w_ref[...].astype(BF16)

        acc = jnp.dot(a.astype(BF16), wb_ref[...], preferred_element_type=F32)
    if bias:
        acc = acc + rest[0][...]
    if rope:
        cos, sin = rest[0][...], rest[1][...]
        for c in range(acc.shape[1] // HEAD_DIM):
            blk = acc[:, c * HEAD_DIM:(c + 1) * HEAD_DIM]
            o_ref[:, c * HEAD_DIM:(c + 1) * HEAD_DIM] = blk * cos + pltpu.roll(blk, HEAD_DIM // 2, 1) * sin
    else:
        o_ref[...] = acc


def _matmul(a, w, col0, ncols, *, tm, tn, rope=None, bias=None, silu_a=False, precise=False):
    m, k = a.shape
    assert precise == (a.dtype == F32)
    assert m % tm == 0 and ncols % tn == 0 and col0 % tn == 0
    assert rope is None or bias is None
    cb = col0 // tn
    in_specs = [pl.BlockSpec((tm, k), lambda j, i: (i, 0)),
                pl.BlockSpec((k, tn), lambda j, i: (0, cb + j))]
    args = [a, w]
    if rope is not None:
        in_specs += [pl.BlockSpec((tm, HEAD_DIM), lambda j, i: (i, 0))] * 2
        args += list(rope)
    if bias is not None:
        in_specs.append(pl.BlockSpec((1, tn), lambda j, i: (0, cb + j)))
        args.append(bias)
    return pl.pallas_call(
        functools.partial(_mm_kernel, rope=rope is not None, bias=bias is not None, silu_a=silu_a,
                          precise=precise),
        grid=(ncols // tn, m // tm),
        in_specs=in_specs,
        out_specs=pl.BlockSpec((tm, tn), lambda j, i: (i, j)),
        out_shape=jax.ShapeDtypeStruct((m, ncols), F32),
        scratch_shapes=[] if precise else [pltpu.VMEM((k, tn), BF16)],
        compiler_params=_cparams(("arbitrary", "arbitrary")),
    )(*args)


def _prenorm_kernel(x_ref, g_ref, scale_ref, shift_ref, h_ref):
    y = _rms(x_ref[...], NORM_EPS) * g_ref[...]
    h_ref[...] = (y * (1.0 + scale_ref[...]) + shift_ref[...]).astype(h_ref.dtype)


def _row_spec(per_row, tm, d):
    return pl.BlockSpec((tm, d), lambda i: (i, 0)) if per_row else pl.BlockSpec((1, d), lambda i: (0, 0))


def _prenorm(x, g, scale, shift, *, tm, dtype=BF16):
    m, d = x.shape
    per_row = scale.shape[0] != 1
    return pl.pallas_call(
        _prenorm_kernel,
        grid=(m // tm,),
        in_specs=[pl.BlockSpec((tm, d), lambda i: (i, 0)), pl.BlockSpec((1, d), lambda i: (0, 0)),
                  _row_spec(per_row, tm, d), _row_spec(per_row, tm, d)],
        out_specs=pl.BlockSpec((tm, d), lambda i: (i, 0)),
        out_shape=jax.ShapeDtypeStruct((m, d), dtype),
        compiler_params=_cparams(("arbitrary",)),
    )(x, g, scale, shift)


def _merge_kernel(ga_ref, gb_ref, oa_ref, ob_ref, o_ref):
    o_ref[...] = (jax.nn.sigmoid(ga_ref[...]) * oa_ref[...]
                  + jax.nn.sigmoid(gb_ref[...]) * ob_ref[...]).astype(o_ref.dtype)


def _merge(gates, o_a, o_b, *, tm, dtype=BF16):
    m, d = o_a.shape
    tc = min(d, 1024)
    nc = d // tc
    return pl.pallas_call(
        _merge_kernel,
        grid=(m // tm, nc),
        in_specs=[pl.BlockSpec((tm, tc), lambda i, j: (i, j)),
                  pl.BlockSpec((tm, tc), lambda i, j: (i, nc + j)),
                  pl.BlockSpec((tm, tc), lambda i, j: (i, j)),
                  pl.BlockSpec((tm, tc), lambda i, j: (i, j))],
        out_specs=pl.BlockSpec((tm, tc), lambda i, j: (i, j)),
        out_shape=jax.ShapeDtypeStruct((m, d), dtype),
        compiler_params=_cparams(("arbitrary", "arbitrary")),
    )(gates, gates, o_a, o_b)


def _post_mix_kernel(x_ref, z_ref, gate_ref, gpost_ref, gpre_ref, scale_ref, shift_ref, wr_ref, br_ref,
                     x1_ref, h2_ref, lg_ref):
    x1 = x_ref[...] + gate_ref[...] * (_rms(z_ref[...], NORM_EPS) * gpost_ref[...])
    x1_ref[...] = x1
    h2 = _rms(x1, NORM_EPS) * gpre_ref[...] * (1.0 + scale_ref[...]) + shift_ref[...]
    h2_ref[...] = h2.astype(h2_ref.dtype)
    lg_ref[...] = jnp.dot(h2, wr_ref[...], precision=lax.Precision.HIGHEST,
                          preferred_element_type=F32) + br_ref[...]


def _post_mix(x, z, gate, g_post, g_pre, scale, shift, w_router, b_router, *, tm):
    m, d = x.shape
    per_row = gate.shape[0] != 1
    ne = w_router.shape[1]
    row = pl.BlockSpec((tm, d), lambda i: (i, 0))
    vec = pl.BlockSpec((1, d), lambda i: (0, 0))
    mod = _row_spec(per_row, tm, d)
    return pl.pallas_call(
        _post_mix_kernel,
        grid=(m // tm,),
        in_specs=[row, row, mod, vec, vec, mod, mod,
                  pl.BlockSpec((d, ne), lambda i: (0, 0)), pl.BlockSpec((1, ne), lambda i: (0, 0))],
        out_specs=[row, row, pl.BlockSpec((tm, ne), lambda i: (i, 0))],
        out_shape=[jax.ShapeDtypeStruct((m, d), F32), jax.ShapeDtypeStruct((m, d), BF16),
                   jax.ShapeDtypeStruct((m, ne), F32)],
        compiler_params=_cparams(("arbitrary",)),
    )(x, z, gate, g_post, g_pre, scale, shift, w_router, b_router)


def _post_ffn_kernel(x_ref, z_ref, gate_ref, gpost_ref, y_ref):
    y_ref[...] = x_ref[...] + gate_ref[...] * (_rms(z_ref[...], NORM_EPS) * gpost_ref[...])


def _post_ffn(x, z, gate, g_post, *, tm):
    m, d = x.shape
    per_row = gate.shape[0] != 1
    row = pl.BlockSpec((tm, d), lambda i: (i, 0))
    return pl.pallas_call(
        _post_ffn_kernel,
        grid=(m // tm,),
        in_specs=[row, row, _row_spec(per_row, tm, d), pl.BlockSpec((1, d), lambda i: (0, 0))],
        out_specs=row,
        out_shape=jax.ShapeDtypeStruct((m, d), F32),
        compiler_params=_cparams(("arbitrary",)),
    )(x, z, gate, g_post)


def _topk_mask(score, k, n_valid_lanes):
    lane = lax.broadcasted_iota(jnp.int32, score.shape, 1)
    sel = jnp.zeros(score.shape, F32)
    vals, idxs = [], []
    for _ in range(k):
        m = jnp.max(score, axis=1, keepdims=True)
        idx = jnp.min(jnp.where(score == m, lane, n_valid_lanes), axis=1, keepdims=True)
        pick = lane == idx
        sel = jnp.where(pick & (m > 0.5 * NEG_INF), 1.0, sel)
        score = jnp.where(pick, -jnp.inf, score)
        vals.append(m)
        idxs.append(idx)
    return sel, vals, idxs


def _stack_heads(x, n, width, offset=0, stride=None):
    stride = width if stride is None else stride
    return jnp.concatenate([x[:, h * stride + offset:h * stride + offset + width] for h in range(n)], axis=0)


def _softmax_step(s, v, m_ref, l_ref, acc_ref):
    m_prev = m_ref[...]
    m_new = jnp.maximum(m_prev, jnp.max(s, axis=1, keepdims=True))
    alpha = jnp.exp(m_prev - m_new)
    p = jnp.exp(s - m_new)
    l_ref[...] = alpha * l_ref[...] + jnp.sum(p, axis=1, keepdims=True)
    acc_ref[...] = alpha * acc_ref[...] + jnp.dot(p.astype(BF16), v, preferred_element_type=F32)
    m_ref[...] = m_new


def _causal_mask(rows, blk):
    r = lax.broadcasted_iota(jnp.int32, (rows, blk), 0) & (blk - 1)
    c = lax.broadcasted_iota(jnp.int32, (rows, blk), 1)
    return c <= r


def _qk(q, k):
    return lax.dot_general(q, k, (((1,), (1,)), ((), ())), preferred_element_type=F32)


def _moba_prompt_kernel(q_ref, k_ref, v_ref, o_ref, kb_ref, vb_ref, kmean_ref, m_ref, l_ref, acc_ref):
    n = pl.program_id(1)
    blk = MOBA_BLOCK
    nb = k_ref.shape[0] // blk
    scale = HEAD_DIM ** -0.5

    @pl.when(n == 0)
    def _():
        kb_ref[...] = k_ref[...].astype(BF16)
        vb_ref[...] = v_ref[...].astype(BF16)
        for b in range(nb):
            kmean_ref[b:b + 1, :] = jnp.mean(k_ref[b * blk:(b + 1) * blk, :], axis=0, keepdims=True)

    q4 = _stack_heads(q_ref[...], MOBA_GROUP, HEAD_DIM)
    gate = lax.dot_general(q4, kmean_ref[...], (((1,), (1,)), ((), ())),
                           precision=lax.Precision.HIGHEST, preferred_element_type=F32)
    lane = lax.broadcasted_iota(jnp.int32, gate.shape, 1)
    gate = jnp.where(lane < n, gate, NEG_INF)
    sel, _, _ = _topk_mask(gate, min(MOBA_TOPK, nb - 1), nb)

    qb = (q4 * scale).astype(BF16)
    own = pl.multiple_of(n * blk, blk)
    s = _qk(qb, kb_ref[pl.ds(own, blk), :])
    s = jnp.where(_causal_mask(MOBA_GROUP * blk, blk), s, NEG_INF)
    m0 = jnp.max(s, axis=1, keepdims=True)
    p = jnp.exp(s - m0)
    m_ref[...] = m0
    l_ref[...] = jnp.sum(p, axis=1, keepdims=True)
    acc_ref[...] = jnp.dot(p.astype(BF16), vb_ref[pl.ds(own, blk), :], preferred_element_type=F32)

    def body(j, carry):
        picked = jnp.sum(jnp.where(lane == j, sel, 0.0), axis=1, keepdims=True)
        off = pl.multiple_of(j * blk, blk)
        sj = _qk(qb, kb_ref[pl.ds(off, blk), :])
        sj = jnp.where(picked > 0.0, sj, NEG_INF)
        _softmax_step(sj, vb_ref[pl.ds(off, blk), :], m_ref, l_ref, acc_ref)
        return carry

    lax.fori_loop(0, n, body, 0)
    out = acc_ref[...] / l_ref[...]
    for h in range(MOBA_GROUP):
        o_ref[:, h * HEAD_DIM:(h + 1) * HEAD_DIM] = out[h * blk:(h + 1) * blk, :]


def _moba_prompt(q, k, v):
    s, dq = q.shape
    kv = k.shape[1] // HEAD_DIM
    blk = MOBA_BLOCK
    assert s % blk == 0 and dq == kv * MOBA_GROUP * HEAD_DIM
    rows = MOBA_GROUP * blk
    return pl.pallas_call(
        _moba_prompt_kernel,
        grid=(kv, s // blk),
        in_specs=[pl.BlockSpec((blk, MOBA_GROUP * HEAD_DIM), lambda g, n: (n, g)),
                  pl.BlockSpec((s, HEAD_DIM), lambda g, n: (0, g)),
                  pl.BlockSpec((s, HEAD_DIM), lambda g, n: (0, g))],
        out_specs=pl.BlockSpec((blk, MOBA_GROUP * HEAD_DIM), lambda g, n: (n, g)),
        out_shape=jax.ShapeDtypeStruct((s, dq), F32),
        scratch_shapes=[pltpu.VMEM((s, HEAD_DIM), BF16), pltpu.VMEM((s, HEAD_DIM), BF16),
                        pltpu.VMEM((s // blk, HEAD_DIM), F32),
                        pltpu.VMEM((rows, 1), F32), pltpu.VMEM((rows, 1), F32),
                        pltpu.VMEM((rows, HEAD_DIM), F32)],
        compiler_params=_cparams(("arbitrary", "arbitrary")),
    )(q, k, v)


def _lambda_full(lq1_ref, lk1_ref, lq2_ref, lk2_ref, lam_init):
    return (jnp.exp(jnp.sum(lq1_ref[...] * lk1_ref[...], axis=1, keepdims=True))
            - jnp.exp(jnp.sum(lq2_ref[...] * lk2_ref[...], axis=1, keepdims=True)) + lam_init)


def _diff_prompt_kernel(q_ref, k_ref, v_ref, lq1_ref, lk1_ref, lq2_ref, lk2_ref, sub_ref, o_ref,
                        m1_ref, l1_ref, acc1_ref, m2_ref, l2_ref, acc2_ref, *, lam_init, tq):
    n = pl.program_id(1)
    scale = HEAD_DIM ** -0.5
    dv = 2 * HEAD_DIM
    q = q_ref[...] * scale
    q1 = _stack_heads(q, DIFF_GROUP, HEAD_DIM, 0, dv).astype(BF16)
    q2 = _stack_heads(q, DIFF_GROUP, HEAD_DIM, HEAD_DIM, dv).astype(BF16)
    for m_ref, l_ref, acc_ref in ((m1_ref, l1_ref, acc1_ref), (m2_ref, l2_ref, acc2_ref)):
        m_ref[...] = jnp.full(m_ref.shape, NEG_INF, F32)
        l_ref[...] = jnp.zeros(l_ref.shape, F32)
        acc_ref[...] = jnp.zeros(acc_ref.shape, F32)

    def step(off, mask):
        kj = k_ref[pl.ds(off, tq), :]
        vj = v_ref[pl.ds(off, tq), :]
        s1 = _qk(q1, kj[:, :HEAD_DIM])
        s2 = _qk(q2, kj[:, HEAD_DIM:])
        if mask is not None:
            s1 = jnp.where(mask, s1, NEG_INF)
            s2 = jnp.where(mask, s2, NEG_INF)
        _softmax_step(s1, vj, m1_ref, l1_ref, acc1_ref)
        _softmax_step(s2, vj, m2_ref, l2_ref, acc2_ref)

    def body(j, carry):
        step(pl.multiple_of(j * tq, tq), None)
        return carry

    lax.fori_loop(0, n, body, 0)
    step(pl.multiple_of(n * tq, tq), _causal_mask(DIFF_GROUP * tq, tq))

    lam = _lambda_full(lq1_ref, lk1_ref, lq2_ref, lk2_ref, lam_init)
    o = acc1_ref[...] / l1_ref[...] - lam * (acc2_ref[...] / l2_ref[...])
    o = _rms(o, SUBLN_EPS) * sub_ref[...] * (1.0 - lam_init)
    for h in range(DIFF_GROUP):
        o_ref[:, h * dv:(h + 1) * dv] = o[h * tq:(h + 1) * tq, :]


def _diff_prompt(q, k, v, lam_params, subln, lam_init, *, tq=256):
    s, dq = q.shape
    dv = 2 * HEAD_DIM
    kv = k.shape[1] // dv
    assert s % tq == 0 and dq == kv * DIFF_GROUP * dv
    rows = DIFF_GROUP * tq
    vec = pl.BlockSpec((1, HEAD_DIM), lambda g, n: (0, 0))
    stat = pltpu.VMEM((rows, 1), F32)
    acc = pltpu.VMEM((rows, dv), F32)
    return pl.pallas_call(
        functools.partial(_diff_prompt_kernel, lam_init=lam_init, tq=tq),
        grid=(kv, s // tq),
        in_specs=[pl.BlockSpec((tq, DIFF_GROUP * dv), lambda g, n: (n, g)),
                  pl.BlockSpec((s, dv), lambda g, n: (0, g)),
                  pl.BlockSpec((s, dv), lambda g, n: (0, g)),
                  vec, vec, vec, vec, pl.BlockSpec((1, dv), lambda g, n: (0, 0))],
        out_specs=pl.BlockSpec((tq, DIFF_GROUP * dv), lambda g, n: (n, g)),
        out_shape=jax.ShapeDtypeStruct((s, dq), F32),
        scratch_shapes=[stat, stat, acc, stat, stat, acc],
        compiler_params=_cparams(("arbitrary", "arbitrary")),
    )(q, k, v, *lam_params, subln)


def _load_heads(ref, n_heads, rows):
    return jnp.concatenate([ref[0, pl.ds(h, rows, stride=n_heads), :] for h in range(n_heads)], axis=1)


def _diag_blocks(x, width, group, n_groups):
    out = jnp.zeros((x.shape[0], width), x.dtype)
    for g in range(n_groups):
        out = jnp.where(group == g, x[:, g * width:(g + 1) * width], out)
    return out


def _page_scores(q, k0, k1):
    qk = lambda k: lax.dot_general(q, k, (((1,), (1,)), ((), ())), precision=lax.Precision.HIGHEST,
                                   preferred_element_type=F32)
    return jnp.concatenate([qk(k0), qk(k1)], axis=1)


def _page_pv(p, v0, v1):
    half = p.shape[1] // 2
    pv = lambda a, v: jnp.dot(a, v, precision=lax.Precision.HIGHEST, preferred_element_type=F32)
    return pv(p[:, :half], v0) + pv(p[:, half:], v1)


def _moba_decode_kernel(pt_ref, qbd_ref, qf_ref, knew_ref, vnew_ref, k0_ref, k1_ref, v0_ref, v1_ref, o_ref,
                        m_ref, l_ref, g_ref, oall_ref, *, nblk, kv):
    del pt_ref
    j = pl.program_id(1)
    scale = HEAD_DIM ** -0.5

    @pl.when(j == 0)
    def _():
        m_ref[...] = jnp.full(m_ref.shape, NEG_INF, F32)
        l_ref[...] = jnp.zeros(l_ref.shape, F32)
        g_ref[...] = jnp.full(g_ref.shape, NEG_INF, F32)

    qbd = qbd_ref[0]
    k0 = _load_heads(k0_ref, kv, PAGE_SIZE)
    k1 = _load_heads(k1_ref, kv, PAGE_SIZE)
    s = _page_scores(qbd * scale, k0, k1)
    ksum = jnp.sum(k0, axis=0, keepdims=True) + jnp.sum(k1, axis=0, keepdims=True)
    gate = jnp.sum(qbd * ksum, axis=1, keepdims=True) * (1.0 / MOBA_BLOCK)
    m = jnp.max(s, axis=1, keepdims=True)
    p = jnp.exp(s - m)
    pv = _page_pv(p, _load_heads(v0_ref, kv, PAGE_SIZE), _load_heads(v1_ref, kv, PAGE_SIZE))
    nh = qbd.shape[0]
    hgrp = lax.broadcasted_iota(jnp.int32, (nh, HEAD_DIM), 0) // MOBA_GROUP
    lane = lax.broadcasted_iota(jnp.int32, (nh, LANES), 1)
    m_ref[...] = jnp.where(lane == j, m, m_ref[...])
    l_ref[...] = jnp.where(lane == j, jnp.sum(p, axis=1, keepdims=True), l_ref[...])
    g_ref[...] = jnp.where(lane == j, gate, g_ref[...])
    oall_ref[j] = _diag_blocks(pv, HEAD_DIM, hgrp, kv)

    @pl.when(j == nblk - 1)
    def _():
        sel, _, _ = _topk_mask(g_ref[...], min(MOBA_TOPK, nblk), LANES)
        s_new = jnp.sum(qf_ref[0] * knew_ref[0], axis=1, keepdims=True) * scale
        m_all = m_ref[...]
        mx = jnp.maximum(jnp.max(jnp.where(sel > 0.0, m_all, NEG_INF), axis=1, keepdims=True), s_new)
        w = jnp.where(sel > 0.0, jnp.exp(m_all - mx), 0.0)
        e_new = jnp.exp(s_new - mx)
        den = jnp.sum(w * l_ref[...], axis=1, keepdims=True) + e_new
        num = e_new * vnew_ref[0]
        for b in range(nblk):
            num = num + w[:, b:b + 1] * oall_ref[b]
        o_ref[0] = num / den


def _paged_specs(npages, rows_per_page):
    def spec(which):
        return pl.BlockSpec((1, rows_per_page, HEAD_DIM), lambda b, j, pt: (pt[b * npages + 2 * j + which], 0, 0))
    return spec(0), spec(1)


def _moba_decode(page_table, qbd, qf, knew, vnew, cache_k, cache_v):
    nseq, nh, _ = qf.shape
    npages = page_table.shape[1]
    kv = cache_k.shape[1] // PAGE_SIZE
    nblk = npages * PAGE_SIZE // MOBA_BLOCK
    assert MOBA_BLOCK == 2 * PAGE_SIZE and nblk <= LANES
    p0, p1 = _paged_specs(npages, PAGE_SIZE * kv)
    per_seq = lambda w: pl.BlockSpec((1, nh, w), lambda b, j, pt: (b, 0, 0))
    gs = pltpu.PrefetchScalarGridSpec(
        num_scalar_prefetch=1, grid=(nseq, nblk),
        in_specs=[per_seq(kv * HEAD_DIM), per_seq(HEAD_DIM), per_seq(HEAD_DIM), per_seq(HEAD_DIM),
                  p0, p1, p0, p1],
        out_specs=per_seq(HEAD_DIM),
        scratch_shapes=[pltpu.VMEM((nh, LANES), F32), pltpu.VMEM((nh, LANES), F32), pltpu.VMEM((nh, LANES), F32),
                        pltpu.VMEM((nblk, nh, HEAD_DIM), F32)])
    return pl.pallas_call(
        functools.partial(_moba_decode_kernel, nblk=nblk, kv=kv),
        grid_spec=gs,
        out_shape=jax.ShapeDtypeStruct((nseq, nh, HEAD_DIM), F32),
        compiler_params=_cparams(("arbitrary", "arbitrary")),
    )(page_table.reshape(-1), qbd, qf, knew, vnew, cache_k, cache_k, cache_v, cache_v)


def _diff_decode_kernel(pt_ref, qbd_ref, qf_ref, knew_ref, vnew_ref, k0_ref, k1_ref, v0_ref, v1_ref,
                        lq1_ref, lk1_ref, lq2_ref, lk2_ref, sub_ref, o_ref, m_ref, l_ref, acc_ref,
                        *, nsteps, kv, lam_init):
    del pt_ref
    j = pl.program_id(1)
    scale = HEAD_DIM ** -0.5
    dv = 2 * HEAD_DIM

    @pl.when(j == 0)
    def _():
        m_ref[...] = jnp.full(m_ref.shape, NEG_INF, F32)
        l_ref[...] = jnp.zeros(l_ref.shape, F32)
        acc_ref[...] = jnp.zeros(acc_ref.shape, F32)

    q = qbd_ref[0] * scale
    s = _page_scores(q, _load_heads(k0_ref, 2 * kv, PAGE_SIZE), _load_heads(k1_ref, 2 * kv, PAGE_SIZE))
    v0 = jnp.concatenate([v0_ref[0, :, g, :] for g in range(kv)], axis=1)
    v1 = jnp.concatenate([v1_ref[0, :, g, :] for g in range(kv)], axis=1)
    m_prev = m_ref[...]
    m_new = jnp.maximum(m_prev, jnp.max(s, axis=1, keepdims=True))
    alpha = jnp.exp(m_prev - m_new)
    p = jnp.exp(s - m_new)
    l_ref[...] = alpha * l_ref[...] + jnp.sum(p, axis=1, keepdims=True)
    acc_ref[...] = alpha * acc_ref[...] + _page_pv(p, v0, v1)
    m_ref[...] = m_new

    @pl.when(j == nsteps - 1)
    def _():
        rows = acc_ref.shape[0]
        nh = rows // 2
        hgrp = (lax.broadcasted_iota(jnp.int32, (rows, dv), 0) & (nh - 1)) // DIFF_GROUP
        acc = _diag_blocks(acc_ref[...], dv, hgrp, kv)
        s_new = jnp.sum(qf_ref[0] * knew_ref[0], axis=1, keepdims=True) * scale
        mx = jnp.maximum(m_ref[...], s_new)
        a = jnp.exp(m_ref[...] - mx)
        e = jnp.exp(s_new - mx)
        o2 = (a * acc + e * vnew_ref[0]) / (a * l_ref[...] + e)
        lam = _lambda_full(lq1_ref, lk1_ref, lq2_ref, lk2_ref, lam_init)
        o = o2[:nh] - lam * o2[nh:]
        o_ref[0] = _rms(o, SUBLN_EPS) * sub_ref[...] * (1.0 - lam_init)


def _diff_decode(page_table, qbd, qf, knew, vnew, cache_k, cache_v, lam_params, subln, lam_init):
    nseq, rows, _ = qf.shape
    npages = page_table.shape[1]
    kv = cache_v.shape[2]
    dv = 2 * HEAD_DIM
    assert npages % 2 == 0 and rows & (rows - 1) == 0
    nsteps = npages // 2
    k0, k1 = _paged_specs(npages, PAGE_SIZE * 2 * kv)
    vspec = lambda which: pl.BlockSpec((1, PAGE_SIZE, kv, dv),
                                       lambda b, j, pt: (pt[b * npages + 2 * j + which], 0, 0, 0))
    per_seq = lambda r, w: pl.BlockSpec((1, r, w), lambda b, j, pt: (b, 0, 0))
    vec = pl.BlockSpec((1, HEAD_DIM), lambda b, j, pt: (0, 0))
    gs = pltpu.PrefetchScalarGridSpec(
        num_scalar_prefetch=1, grid=(nseq, nsteps),
        in_specs=[per_seq(rows, 2 * kv * HEAD_DIM), per_seq(rows, HEAD_DIM), per_seq(rows, HEAD_DIM),
                  per_seq(rows, dv), k0, k1, vspec(0), vspec(1), vec, vec, vec, vec,
                  pl.BlockSpec((1, dv), lambda b, j, pt: (0, 0))],
        out_specs=per_seq(rows // 2, dv),
        scratch_shapes=[pltpu.VMEM((rows, 1), F32), pltpu.VMEM((rows, 1), F32),
                        pltpu.VMEM((rows, kv * dv), F32)])
    return pl.pallas_call(
        functools.partial(_diff_decode_kernel, nsteps=nsteps, kv=kv, lam_init=lam_init),
        grid_spec=gs,
        out_shape=jax.ShapeDtypeStruct((nseq, rows // 2, dv), F32),
        compiler_params=_cparams(("arbitrary", "arbitrary")),
    )(page_table.reshape(-1), qbd, qf, knew, vnew, cache_k, cache_k, cache_v, cache_v, *lam_params, subln)


def _route_kernel(lg_ref, idx_ref, wts_ref, rank_ref, cnt_ref, *, n_tokens, tm):
    i = pl.program_id(0)

    @pl.when(i == 0)
    def _():
        cnt_ref[...] = jnp.zeros(cnt_ref.shape, F32)

    lg = lg_ref[...]
    ne = lg.shape[1]
    sel, vals, idxs = _topk_mask(lg, TOP_K, ne)
    row = i * tm + lax.broadcasted_iota(jnp.int32, (tm, 1), 0)
    sel = jnp.where(row < n_tokens, sel, 0.0)
    ex = [jnp.exp(v - vals[0]) for v in vals]
    den = functools.reduce(lambda a, b: a + b, ex)
    r = lax.broadcasted_iota(jnp.int32, (tm, tm), 0)
    c = lax.broadcasted_iota(jnp.int32, (tm, tm), 1)
    earlier = jnp.where(c < r, 1.0, 0.0).astype(BF16)
    before = jnp.dot(earlier, sel.astype(BF16), preferred_element_type=F32) + cnt_ref[...]
    lane = lax.broadcasted_iota(jnp.int32, (tm, ne), 1)
    lane_k = lax.broadcasted_iota(jnp.int32, (tm, TOP_K), 1)
    idx_o = jnp.zeros((tm, TOP_K), jnp.int32)
    rank_o = jnp.zeros((tm, TOP_K), jnp.int32)
    w_o = jnp.zeros((tm, TOP_K), F32)
    for k in range(TOP_K):
        rank = jnp.sum(jnp.where(lane == idxs[k], before, 0.0), axis=1, keepdims=True)
        idx_o = jnp.where(lane_k == k, idxs[k], idx_o)
        rank_o = jnp.where(lane_k == k, rank.astype(jnp.int32), rank_o)
        w_o = jnp.where(lane_k == k, ex[k] / den, w_o)
    idx_ref[...] = idx_o
    rank_ref[...] = rank_o
    wts_ref[...] = w_o
    cnt_ref[...] = cnt_ref[...] + jnp.sum(sel, axis=0, keepdims=True)


def _route(logits, n_tokens):
    npad, ne = logits.shape
    tm = ROUTE_TM
    out4 = pl.BlockSpec((tm, TOP_K), lambda i: (i, 0))
    return pl.pallas_call(
        functools.partial(_route_kernel, n_tokens=n_tokens, tm=tm),
        grid=(npad // tm,),
        in_specs=[pl.BlockSpec((tm, ne), lambda i: (i, 0))],
        out_specs=[out4, out4, out4, pl.BlockSpec((1, ne), lambda i: (0, 0))],
        out_shape=[jax.ShapeDtypeStruct((npad, TOP_K), jnp.int32), jax.ShapeDtypeStruct((npad, TOP_K), F32),
                   jax.ShapeDtypeStruct((npad, TOP_K), jnp.int32), jax.ShapeDtypeStruct((1, ne), F32)],
        compiler_params=_cparams(("arbitrary",)),
    )(logits)


def _scatter_kernel(pos_ref, h_ref, xin_ref, xout_ref, sem, *, n_tokens):
    del xin_ref

    def copy(t, k):
        return pltpu.make_async_copy(h_ref.at[t], xout_ref.at[pos_ref[TOP_K * t + k]], sem)

    def body(t, carry):
        for k in range(TOP_K):
            copy(t, k).start()

        @pl.when(t >= DMA_WINDOW)
        def _():
            for k in range(TOP_K):
                copy(t, k).wait()
        return carry

    lax.fori_loop(0, n_tokens, body, 0)

    def drain(t, carry):
        for k in range(TOP_K):
            copy(t, k).wait()
        return carry

    lax.fori_loop(0, min(DMA_WINDOW, n_tokens), drain, 0)


def _scatter_rows(pos_flat, h3, n_rows, n_tokens):
    _, s, l = h3.shape
    zeros = jnp.zeros((n_rows, s, l), h3.dtype)
    anyspec = pl.BlockSpec(memory_space=pl.ANY)
    gs = pltpu.PrefetchScalarGridSpec(
        num_scalar_prefetch=1, grid=(1,), in_specs=[anyspec, anyspec], out_specs=anyspec,
        scratch_shapes=[pltpu.SemaphoreType.DMA(())])
    return pl.pallas_call(
        functools.partial(_scatter_kernel, n_tokens=n_tokens),
        grid_spec=gs,
        out_shape=jax.ShapeDtypeStruct((n_rows, s, l), h3.dtype),
        input_output_aliases={2: 0},
        compiler_params=_cparams(("arbitrary",)),
    )(pos_flat, h3, zeros)


def _swiglu_pairs(u):
    gl = jnp.minimum(u, SWIGLU_LIMIT)
    lin = jnp.clip(pltpu.roll(u, LANES - 1, 1), -SWIGLU_LIMIT, SWIGLU_LIMIT)
    even = (lax.broadcasted_iota(jnp.int32, u.shape, 1) & 1) == 0
    return jnp.where(even, gl * jax.nn.sigmoid(SWIGLU_ALPHA * gl) * (lin + 1.0), 0.0)


def _expert_kernel(gblk_ref, gexp_ref, grows_ref, x_ref, w1_ref, b1_ref, w2_ref, b2_ref, o_ref,
                   wb_ref, act_ref, *, n1):
    del gblk_ref, gexp_ref
    g = pl.program_id(0)
    s = pl.program_id(1)
    rows = grows_ref[g]
    tn = EXPERT_TN
    sub = EXPERT_SUB
    half = tn // 2
    d_in = x_ref.shape[1]
    d_ff = w2_ref.shape[1]

    @pl.when((rows > 0) & (s < n1))
    def _():
        wb_ref[0:d_in, :] = w1_ref[0].astype(BF16)
        rr = lax.broadcasted_iota(jnp.int32, (tn, half), 0)
        cc = lax.broadcasted_iota(jnp.int32, (tn, half), 1)
        pick_even = jnp.where(rr == 2 * cc, 1.0, 0.0).astype(BF16)
        for t in range(EXPERT_ROWS // sub):
            @pl.when(t * sub < rows)
            def _():
                u = jnp.dot(x_ref[t * sub:(t + 1) * sub, :], wb_ref[0:d_in, :],
                            preferred_element_type=F32) + b1_ref[0]
                a = jnp.concatenate([_swiglu_pairs(u[:, c * LANES:(c + 1) * LANES])
                                     for c in range(tn // LANES)], axis=1).astype(BF16)
                a = jnp.dot(a, pick_even, preferred_element_type=F32).astype(BF16)

                @pl.when(s % 2 == 0)
                def _():
                    act_ref[s // 2, t * sub:(t + 1) * sub, 0:half] = a

                @pl.when(s % 2 == 1)
                def _():
                    act_ref[s // 2, t * sub:(t + 1) * sub, half:tn] = a

    @pl.when((rows > 0) & (s >= n1))
    def _():
        wb_ref[0:d_ff, :] = w2_ref[0].astype(BF16)
        for t in range(EXPERT_ROWS // sub):
            @pl.when(t * sub < rows)
            def _():
                acc = jnp.zeros((sub, tn), F32) + b2_ref[0]
                for c in range(d_ff // tn):
                    acc = acc + jnp.dot(act_ref[c, t * sub:(t + 1) * sub, :], wb_ref[c * tn:(c + 1) * tn, :],
                                        preferred_element_type=F32)
                o_ref[t * sub:(t + 1) * sub, :] = acc


def _experts(gblk, gexp, grows, xs, w1, b1, w2, b2):
    n_rows, d = xs.shape
    ne, _, f2 = w1.shape
    d_ff = w2.shape[1]
    tn = EXPERT_TN
    n1, n2 = f2 // tn, d // tn
    last = n1 + n2 - 1
    assert f2 == 2 * d_ff and n1 % 2 == 0 and d_ff % tn == 0 and n_rows % EXPERT_ROWS == 0

    def step(g, s, gr):
        return jnp.where(gr[g] > 0, s, last)

    gs = pltpu.PrefetchScalarGridSpec(
        num_scalar_prefetch=3, grid=(n_rows // EXPERT_ROWS, n1 + n2),
        in_specs=[
            pl.BlockSpec((EXPERT_ROWS, d), lambda g, s, gb, ge, gr: (gb[g], 0)),
            pl.BlockSpec((1, d, tn), lambda g, s, gb, ge, gr: (ge[g], 0, jnp.minimum(step(g, s, gr), n1 - 1))),
            pl.BlockSpec((1, 1, tn), lambda g, s, gb, ge, gr: (ge[g], 0, jnp.minimum(step(g, s, gr), n1 - 1))),
            pl.BlockSpec((1, d_ff, tn), lambda g, s, gb, ge, gr: (ge[g], 0, jnp.maximum(step(g, s, gr) - n1, 0))),
            pl.BlockSpec((1, 1, tn), lambda g, s, gb, ge, gr: (ge[g], 0, jnp.maximum(step(g, s, gr) - n1, 0))),
        ],
        out_specs=pl.BlockSpec((EXPERT_ROWS, tn),
                               lambda g, s, gb, ge, gr: (gb[g], jnp.maximum(step(g, s, gr) - n1, 0))),
        scratch_shapes=[pltpu.VMEM((max(d, d_ff), tn), BF16),
                        pltpu.VMEM((n1 // 2, EXPERT_ROWS, tn), BF16)])
    return pl.pallas_call(
        functools.partial(_expert_kernel, n1=n1),
        grid_spec=gs,
        out_shape=jax.ShapeDtypeStruct((n_rows, d), F32),
        compiler_params=_cparams(("arbitrary", "arbitrary")),
    )(gblk, gexp, grows, xs, w1, b1.reshape(ne, 1, f2), w2, b2.reshape(ne, 1, d))


def _combine_kernel(pos_ref, w_ref, y_ref, o_ref, buf_ref, sem, *, tm):
    base = pl.program_id(0) * tm

    def copy(r, k):
        return pltpu.make_async_copy(y_ref.at[pos_ref[TOP_K * (base + r) + k]], buf_ref.at[k, r], sem)

    def start(r, carry):
        for k in range(TOP_K):
            copy(r, k).start()
        return carry

    def wait(r, carry):
        for k in range(TOP_K):
            copy(r, k).wait()
        return carry

    lax.fori_loop(0, tm, start, 0)
    lax.fori_loop(0, tm, wait, 0)

    def mix(r, carry):
        acc = w_ref[TOP_K * (base + r)] * buf_ref[0, r]
        for k in range(1, TOP_K):
            acc = acc + w_ref[TOP_K * (base + r) + k] * buf_ref[k, r]
        o_ref[r] = acc
        return carry

    lax.fori_loop(0, tm, mix, 0)


def _combine(pos_flat, w_flat, y3, n_pad):
    _, s, l = y3.shape
    tm = COMBINE_TM
    gs = pltpu.PrefetchScalarGridSpec(
        num_scalar_prefetch=1, grid=(n_pad // tm,),
        in_specs=[pl.BlockSpec(memory_space=pltpu.SMEM), pl.BlockSpec(memory_space=pl.ANY)],
        out_specs=pl.BlockSpec((tm, s, l), lambda i, pos: (i, 0, 0)),
        scratch_shapes=[pltpu.VMEM((TOP_K, tm, s, l), F32), pltpu.SemaphoreType.DMA(())])
    return pl.pallas_call(
        functools.partial(_combine_kernel, tm=tm),
        grid_spec=gs,
        out_shape=jax.ShapeDtypeStruct((n_pad, s, l), F32),
        compiler_params=_cparams(("arbitrary",)),
    )(pos_flat, w_flat, y3)


def _moe(h2, logits, w1, b1, w2, b2):
    n, d = h2.shape
    ne = logits.shape[1]
    n_pad = -(-n // ROUTE_TM) * ROUTE_TM
    assert ROUTE_TM % COMBINE_TM == 0 and d % LANES == 0
    idx, wts, rank, cnt = _route(jnp.pad(logits, ((0, n_pad - n), (0, 0))), n)

    cnt = cnt[0].astype(jnp.int32)
    nblk = (cnt + EXPERT_ROWS - 1) // EXPERT_ROWS
    blk_end = jnp.cumsum(nblk)
    blk_start = blk_end - nblk
    n_groups = (TOP_K * n) // EXPERT_ROWS + ne
    g = jnp.arange(n_groups, dtype=jnp.int32)
    used = g < blk_end[-1]
    gblk = jnp.where(used, g, blk_end[-1] - 1)
    gexp = jnp.minimum(jnp.searchsorted(blk_end, gblk, side="right"), ne - 1).astype(jnp.int32)
    grows = jnp.where(used, jnp.clip(cnt[gexp] - (gblk - blk_start[gexp]) * EXPERT_ROWS, 0, EXPERT_ROWS), 0)
    valid = (jnp.arange(n_pad) < n)[:, None]
    pos = jnp.where(valid, (blk_start * EXPERT_ROWS)[idx] + rank, 0).reshape(-1).astype(jnp.int32)
    w_flat = jnp.where(valid, wts, 0.0).reshape(-1)

    n_rows = n_groups * EXPERT_ROWS
    xs3 = _scatter_rows(pos, h2.reshape(n, d // LANES, LANES), n_rows, n)
    ys = _experts(gblk.astype(jnp.int32), gexp, grows.astype(jnp.int32), xs3.reshape(n_rows, d), w1, b1, w2, b2)
    out3 = _combine(pos, w_flat, ys.reshape(n_rows, d // LANES, LANES), n_pad)
    return out3.reshape(n_pad, d)


def _rope_tables(pos):
    half = HEAD_DIM // 2
    inv_freq = 1.0 / (ROPE_THETA ** (jnp.arange(half, dtype=F32) / half))
    ang = pos.astype(F32)[:, None] * inv_freq[None, :]
    cos, sin = jnp.cos(ang), jnp.sin(ang)
    return jnp.concatenate([cos, cos], axis=1), jnp.concatenate([-sin, sin], axis=1)


def _in_proj(h, w_in, rope, d, *, tm, tn, precise=False):
    kvw = d // 4
    mm = functools.partial(_matmul, h, w_in, tm=tm, tn=tn, precise=precise)
    qa = mm(0, d, rope=rope)
    ka = mm(d, kvw, rope=rope)
    va = mm(d + kvw, kvw)
    qb = mm(d + 2 * kvw, d, rope=rope)
    kb = mm(2 * d + 2 * kvw, kvw, rope=rope)
    vb = mm(2 * d + 3 * kvw, kvw)
    gates = mm(2 * d + 4 * kvw, 2 * d)
    return qa, ka, va, qb, kb, vb, gates


def kernel(x_prompt, x_sample, c_prompt, c_sample, cache_moba_k, cache_moba_v, cache_diff_k, cache_diff_v, page_table, w_ada, b_ada, g_pre_mix, g_post_mix, g_pre_ffn, g_post_ffn, w_in, w_out, lambda_q1, lambda_k1, lambda_q2, lambda_k2, diff_subln, w_router, b_router, w_mlp1, b_mlp1, w_mlp2, b_mlp2):
    depth = w_in.shape[0]
    assert depth == 1 and x_prompt.shape[0] == 1 and x_sample.shape[1] == 1
    _, seq, d = x_prompt.shape
    nseq = x_sample.shape[0]
    n_pool = cache_moba_k.shape[1]
    past = page_table.shape[1] * PAGE_SIZE
    lam_init = 0.8 - 0.6 * math.exp(-0.3 * 0)
    row = lambda v: v[0][None, :]
    lam_params = [row(lambda_q1), row(lambda_k1), row(lambda_q2), row(lambda_k2)]
    subln = row(diff_subln)
    nh_a = d // HEAD_DIM
    kv_a = nh_a // MOBA_GROUP
    nh_b = d // (2 * HEAD_DIM)
    kv_b = nh_b // DIFF_GROUP

    c_all = jnp.concatenate([c_prompt, c_sample], axis=0)
    mod = _matmul(c_all, w_ada[0], 0, 6 * d, tm=c_all.shape[0], tn=512, bias=b_ada, silu_a=True, precise=True)
    mods_p = [mod[0:1, i * d:(i + 1) * d] for i in range(6)]
    mods_s = [mod[1:, i * d:(i + 1) * d] for i in range(6)]

    xp = x_prompt[0]
    shift_m, scale_m, gate_m, shift_f, scale_f, gate_f = mods_p
    h = _prenorm(xp, row(g_pre_mix), scale_m, shift_m, tm=256)
    rope_p = _rope_tables(jnp.arange(seq, dtype=jnp.int32))
    qa, ka, va, qb, kb, vb, gates = _in_proj(h, w_in[0], rope_p, d, tm=512, tn=512)
    o_a = _moba_prompt(qa, ka, va)
    o_b = _diff_prompt(qb, kb.astype(BF16), vb.astype(BF16), lam_params, subln, lam_init)
    z = _matmul(_merge(gates, o_a, o_b, tm=256), w_out[0], 0, d, tm=512, tn=512)
    x1_p, h2_p, lg_p = _post_mix(xp, z, gate_m, row(g_post_mix), row(g_pre_ffn), scale_f, shift_f,
                                 w_router[0], b_router, tm=256)
    gate_f_p = gate_f

    xs = x_sample[:, 0, :]
    shift_m, scale_m, gate_m, shift_f, scale_f, gate_f = mods_s
    h = _prenorm(xs, row(g_pre_mix), scale_m, shift_m, tm=nseq, dtype=F32)
    rope_s = _rope_tables(jnp.full((nseq,), past, jnp.int32))
    qa_s, ka_s, va_s, qb_s, kb_s, vb_s, gates_s = _in_proj(h, w_in[0], rope_s, d, tm=nseq, tn=512, precise=True)

    qf_a = qa_s.reshape(nseq, nh_a, HEAD_DIM)
    own_a = (jnp.arange(nh_a) // MOBA_GROUP)[:, None] == jnp.arange(kv_a)[None, :]
    qbd_a = jnp.where(own_a[None, :, :, None], qf_a[:, :, None, :], 0.0).reshape(nseq, nh_a, kv_a * HEAD_DIM)
    per_head = lambda t, width: jnp.repeat(t.reshape(nseq, -1, width), MOBA_GROUP, axis=1)
    o_a_s = _moba_decode(page_table, qbd_a, qf_a, per_head(ka_s, HEAD_DIM), per_head(va_s, HEAD_DIM),
                         cache_moba_k[0].reshape(n_pool, PAGE_SIZE * kv_a, HEAD_DIM),
                         cache_moba_v[0].reshape(n_pool, PAGE_SIZE * kv_a, HEAD_DIM)).reshape(nseq, d)

    qf_b = jnp.swapaxes(qb_s.reshape(nseq, nh_b, 2, HEAD_DIM), 1, 2).reshape(nseq, 2 * nh_b, HEAD_DIM)
    rr = jnp.arange(2 * nh_b)
    kcol = ((rr % nh_b) // DIFF_GROUP) * 2 + rr // nh_b
    own_b = kcol[:, None] == jnp.arange(2 * kv_b)[None, :]
    qbd_b = jnp.where(own_b[None, :, :, None], qf_b[:, :, None, :], 0.0).reshape(nseq, 2 * nh_b, 2 * kv_b * HEAD_DIM)
    knew_b = kb_s.reshape(nseq, 2 * kv_b, HEAD_DIM)[:, kcol, :]
    vnew_b = vb_s.reshape(nseq, kv_b, 2 * HEAD_DIM)[:, (rr % nh_b) // DIFF_GROUP, :]
    o_b_s = _diff_decode(page_table, qbd_b, qf_b, knew_b, vnew_b,
                         cache_diff_k[0].reshape(n_pool, PAGE_SIZE * 2 * kv_b, HEAD_DIM), cache_diff_v[0],
                         lam_params, subln, lam_init).reshape(nseq, d)

    z = _matmul(_merge(gates_s, o_a_s, o_b_s, tm=nseq, dtype=F32), w_out[0], 0, d, tm=nseq, tn=512, precise=True)
    x1_s, h2_s, lg_s = _post_mix(xs, z, gate_m, row(g_post_mix), row(g_pre_ffn), scale_f, shift_f,
                                 w_router[0], b_router, tm=nseq)

    moe = _moe(jnp.concatenate([h2_p, h2_s], axis=0), jnp.concatenate([lg_p, lg_s], axis=0),
               w_mlp1[0], b_mlp1[0], w_mlp2[0], b_mlp2[0])
    y_p = _post_ffn(x1_p, moe[:seq], gate_f_p, row(g_post_ffn), tm=256)
    y_s = _post_ffn(x1_s, moe[seq:seq + nseq], gate_f, row(g_post_ffn), tm=nseq)

    return (y_p[None], y_s[:, None, :],
            ka.reshape(1, 1, seq, kv_a, HEAD_DIM), va.reshape(1, 1, seq, kv_a, HEAD_DIM),
            kb.reshape(1, 1, seq, kv_b, 2, HEAD_DIM), vb.reshape(1, 1, seq, kv_b, 2 * HEAD_DIM),
            ka_s.reshape(1, nseq, 1, kv_a, HEAD_DIM), va_s.reshape(1, nseq, 1, kv_a, HEAD_DIM),
            kb_s.reshape(1, nseq, 1, kv_b, 2, HEAD_DIM), vb_s.reshape(1, nseq, 1, kv_b, 2 * HEAD_DIM))
```

```python
import functools
import math

import jax
import jax.numpy as jnp
from jax import lax
from jax.experimental import pallas as pl
from jax.experimental.pallas import tpu as pltpu

F32 = jnp.float32
BF16 = jnp.bfloat16

HEAD_DIM = 128
MOBA_BLOCK = 256
MOBA_TOPK = 3
MOBA_GROUP = 4
DIFF_GROUP = 4
PAGE_SIZE = 128
ROPE_THETA = 10000.0
TOP_K = 4
SWIGLU_ALPHA = 1.702
SWIGLU_LIMIT = 7.0
NORM_EPS = 1e-6
SUBLN_EPS = 1e-5
NEG_INF = -1e30
LANES = 128
VMEM_LIMIT = 56 * 1024 * 1024

EXPERT_ROWS = 1280
EXPERT_SUB = 256
EXPERT_TN = 256
ROUTE_TM = 256
COMBINE_TM = 64
DMA_WINDOW = 32


def _cparams(sem, vmem=VMEM_LIMIT):
    return pltpu.CompilerParams(dimension_semantics=sem, vmem_limit_bytes=vmem)


def _rms(x, eps):
    return x * lax.rsqrt(jnp.mean(x * x, axis=-1, keepdims=True) + eps)


def _mm_kernel(a_ref, w_ref, *rest, rope, bias, silu_a, precise):
    o_ref = rest[-1] if precise else rest[-2]
    a = a_ref[...]
    if silu_a:
        a = a * jax.nn.sigmoid(a)
    if precise:
        acc = jnp.dot(a, w_ref[...], precision=lax.Precision.HIGHEST, preferred_element_type=F32)
    else:
        wb_ref = rest[-1]

        @pl.when(pl.program_id(1) == 0)
        def _():
            wb_ref[...] = w_ref[...].astype(BF16)

        acc = jnp.dot(a.astype(BF16), wb_ref[...], preferred_element_type=F32)
    if bias:
        acc = acc + rest[0][...]
    if rope:
        cos, sin = rest[0][...], rest[1][...]
        for c in range(acc.shape[1] // HEAD_DIM):
            blk = acc[:, c * HEAD_DIM:(c + 1) * HEAD_DIM]
            o_ref[:, c * HEAD_DIM:(c + 1) * HEAD_DIM] = blk * cos + pltpu.roll(blk, HEAD_DIM // 2, 1) * sin
    else:
        o_ref[...] = acc


def _matmul(a, w, col0, ncols, *, tm, tn, rope=None, bias=None, silu_a=False, precise=False):
    m, k = a.shape
    assert precise == (a.dtype == F32)
    assert m % tm == 0 and ncols % tn == 0 and col0 % tn == 0
    assert rope is None or bias is None
    cb = col0 // tn
    in_specs = [pl.BlockSpec((tm, k), lambda j, i: (i, 0)),
                pl.BlockSpec((k, tn), lambda j, i: (0, cb + j))]
    args = [a, w]
    if rope is not None:
        in_specs += [pl.BlockSpec((tm, HEAD_DIM), lambda j, i: (i, 0))] * 2
        args += list(rope)
    if bias is not None:
        in_specs.append(pl.BlockSpec((1, tn), lambda j, i: (0, cb + j)))
        args.append(bias)
    return pl.pallas_call(
        functools.partial(_mm_kernel, rope=rope is not None, bias=bias is not None, silu_a=silu_a,
                          precise=precise),
        grid=(ncols // tn, m // tm),
        in_specs=in_specs,
        out_specs=pl.BlockSpec((tm, tn), lambda j, i: (i, j)),
        out_shape=jax.ShapeDtypeStruct((m, ncols), F32),
        scratch_shapes=[] if precise else [pltpu.VMEM((k, tn), BF16)],
        compiler_params=_cparams(("arbitrary", "arbitrary")),
    )(*args)


def _prenorm_kernel(x_ref, g_ref, scale_ref, shift_ref, h_ref):
    y = _rms(x_ref[...], NORM_EPS) * g_ref[...]
    h_ref[...] = (y * (1.0 + scale_ref[...]) + shift_ref[...]).astype(h_ref.dtype)


def _row_spec(per_row, tm, d):
    return pl.BlockSpec((tm, d), lambda i: (i, 0)) if per_row else pl.BlockSpec((1, d), lambda i: (0, 0))


def _prenorm(x, g, scale, shift, *, tm, dtype=BF16):
    m, d = x.shape
    per_row = scale.shape[0] != 1
    return pl.pallas_call(
        _prenorm_kernel,
        grid=(m // tm,),
        in_specs=[pl.BlockSpec((tm, d), lambda i: (i, 0)), pl.BlockSpec((1, d), lambda i: (0, 0)),
                  _row_spec(per_row, tm, d), _row_spec(per_row, tm, d)],
        out_specs=pl.BlockSpec((tm, d), lambda i: (i, 0)),
        out_shape=jax.ShapeDtypeStruct((m, d), dtype),
        compiler_params=_cparams(("arbitrary",)),
    )(x, g, scale, shift)


def _merge_kernel(ga_ref, gb_ref, oa_ref, ob_ref, o_ref):
    o_ref[...] = (jax.nn.sigmoid(ga_ref[...]) * oa_ref[...]
                  + jax.nn.sigmoid(gb_ref[...]) * ob_ref[...]).astype(o_ref.dtype)


def _merge(gates, o_a, o_b, *, tm, dtype=BF16):
    m, d = o_a.shape
    tc = min(d, 1024)
    nc = d // tc
    return pl.pallas_call(
        _merge_kernel,
        grid=(m // tm, nc),
        in_specs=[pl.BlockSpec((tm, tc), lambda i, j: (i, j)),
                  pl.BlockSpec((tm, tc), lambda i, j: (i, nc + j)),
                  pl.BlockSpec((tm, tc), lambda i, j: (i, j)),
                  pl.BlockSpec((tm, tc), lambda i, j: (i, j))],
        out_specs=pl.BlockSpec((tm, tc), lambda i, j: (i, j)),
        out_shape=jax.ShapeDtypeStruct((m, d), dtype),
        compiler_params=_cparams(("arbitrary", "arbitrary")),
    )(gates, gates, o_a, o_b)


def _post_mix_kernel(x_ref, z_ref, gate_ref, gpost_ref, gpre_ref, scale_ref, shift_ref, wr_ref, br_ref,
                     x1_ref, h2_ref, lg_ref):
    x1 = x_ref[...] + gate_ref[...] * (_rms(z_ref[...], NORM_EPS) * gpost_ref[...])
    x1_ref[...] = x1
    h2 = _rms(x1, NORM_EPS) * gpre_ref[...] * (1.0 + scale_ref[...]) + shift_ref[...]
    h2_ref[...] = h2.astype(h2_ref.dtype)
    lg_ref[...] = jnp.dot(h2, wr_ref[...], precision=lax.Precision.HIGHEST,
                          preferred_element_type=F32) + br_ref[...]


def _post_mix(x, z, gate, g_post, g_pre, scale, shift, w_router, b_router, *, tm):
    m, d = x.shape
    per_row = gate.shape[0] != 1
    ne = w_router.shape[1]
    row = pl.BlockSpec((tm, d), lambda i: (i, 0))
    vec = pl.BlockSpec((1, d), lambda i: (0, 0))
    mod = _row_spec(per_row, tm, d)
    return pl.pallas_call(
        _post_mix_kernel,
        grid=(m // tm,),
        in_specs=[row, row, mod, vec, vec, mod, mod,
                  pl.BlockSpec((d, ne), lambda i: (0, 0)), pl.BlockSpec((1, ne), lambda i: (0, 0))],
        out_specs=[row, row, pl.BlockSpec((tm, ne), lambda i: (i, 0))],
        out_shape=[jax.ShapeDtypeStruct((m, d), F32), jax.ShapeDtypeStruct((m, d), BF16),
                   jax.ShapeDtypeStruct((m, ne), F32)],
        compiler_params=_cparams(("arbitrary",)),
    )(x, z, gate, g_post, g_pre, scale, shift, w_router, b_router)


def _post_ffn_kernel(x_ref, z_ref, gate_ref, gpost_ref, y_ref):
    y_ref[...] = x_ref[...] + gate_ref[...] * (_rms(z_ref[...], NORM_EPS) * gpost_ref[...])


def _post_ffn(x, z, gate, g_post, *, tm):
    m, d = x.shape
    per_row = gate.shape[0] != 1
    row = pl.BlockSpec((tm, d), lambda i: (i, 0))
    return pl.pallas_call(
        _post_ffn_kernel,
        grid=(m // tm,),
        in_specs=[row, row, _row_spec(per_row, tm, d), pl.BlockSpec((1, d), lambda i: (0, 0))],
        out_specs=row,
        out_shape=jax.ShapeDtypeStruct((m, d), F32),
        compiler_params=_cparams(("arbitrary",)),
    )(x, z, gate, g_post)


def _topk_mask(score, k, n_valid_lanes):
    lane = lax.broadcasted_iota(jnp.int32, score.shape, 1)
    sel = jnp.zeros(score.shape, F32)
    vals, idxs = [], []
    for _ in range(k):
        m = jnp.max(score, axis=1, keepdims=True)
        idx = jnp.min(jnp.where(score == m, lane, n_valid_lanes), axis=1, keepdims=True)
        pick = lane == idx
        sel = jnp.where(pick & (m > 0.5 * NEG_INF), 1.0, sel)
        score = jnp.where(pick, -jnp.inf, score)
        vals.append(m)
        idxs.append(idx)
    return sel, vals, idxs


def _stack_heads(x, n, width, offset=0, stride=None):
    stride = width if stride is None else stride
    return jnp.concatenate([x[:, h * stride + offset:h * stride + offset + width] for h in range(n)], axis=0)


def _softmax_step(s, v, m_ref, l_ref, acc_ref):
    m_prev = m_ref[...]
    m_new = jnp.maximum(m_prev, jnp.max(s, axis=1, keepdims=True))
    alpha = jnp.exp(m_prev - m_new)
    p = jnp.exp(s - m_new)
    l_ref[...] = alpha * l_ref[...] + jnp.sum(p, axis=1, keepdims=True)
    acc_ref[...] = alpha * acc_ref[...] + jnp.dot(p.astype(BF16), v, preferred_element_type=F32)
    m_ref[...] = m_new


def _causal_mask(rows, blk):
    r = lax.broadcasted_iota(jnp.int32, (rows, blk), 0) & (blk - 1)
    c = lax.broadcasted_iota(jnp.int32, (rows, blk), 1)
    return c <= r


def _qk(q, k):
    return lax.dot_general(q, k, (((1,), (1,)), ((), ())), preferred_element_type=F32)


def _moba_prompt_kernel(q_ref, k_ref, v_ref, o_ref, kb_ref, vb_ref, kmean_ref, m_ref, l_ref, acc_ref):
    n = pl.program_id(1)
    blk = MOBA_BLOCK
    nb = k_ref.shape[0] // blk
    scale = HEAD_DIM ** -0.5

    @pl.when(n == 0)
    def _():
        kb_ref[...] = k_ref[...].astype(BF16)
        vb_ref[...] = v_ref[...].astype(BF16)
        for b in range(nb):
            kmean_ref[b:b + 1, :] = jnp.mean(k_ref[b * blk:(b + 1) * blk, :], axis=0, keepdims=True)

    q4 = _stack_heads(q_ref[...], MOBA_GROUP, HEAD_DIM)
    gate = lax.dot_general(q4, kmean_ref[...], (((1,), (1,)), ((), ())),
                           precision=lax.Precision.HIGHEST, preferred_element_type=F32)
    lane = lax.broadcasted_iota(jnp.int32, gate.shape, 1)
    gate = jnp.where(lane < n, gate, NEG_INF)
    sel, _, _ = _topk_mask(gate, min(MOBA_TOPK, nb - 1), nb)

    qb = (q4 * scale).astype(BF16)
    own = pl.multiple_of(n * blk, blk)
    s = _qk(qb, kb_ref[pl.ds(own, blk), :])
    s = jnp.where(_causal_mask(MOBA_GROUP * blk, blk), s, NEG_INF)
    m0 = jnp.max(s, axis=1, keepdims=True)
    p = jnp.exp(s - m0)
    m_ref[...] = m0
    l_ref[...] = jnp.sum(p, axis=1, keepdims=True)
    acc_ref[...] = jnp.dot(p.astype(BF16), vb_ref[pl.ds(own, blk), :], preferred_element_type=F32)

    def body(j, carry):
        picked = jnp.sum(jnp.where(lane == j, sel, 0.0), axis=1, keepdims=True)
        off = pl.multiple_of(j * blk, blk)
        sj = _qk(qb, kb_ref[pl.ds(off, blk), :])
        sj = jnp.where(picked > 0.0, sj, NEG_INF)
        _softmax_step(sj, vb_ref[pl.ds(off, blk), :], m_ref, l_ref, acc_ref)
        return carry

    lax.fori_loop(0, n, body, 0)
    out = acc_ref[...] / l_ref[...]
    for h in range(MOBA_GROUP):
        o_ref[:, h * HEAD_DIM:(h + 1) * HEAD_DIM] = out[h * blk:(h + 1) * blk, :]


def _moba_prompt(q, k, v):
    s, dq = q.shape
    kv = k.shape[1] // HEAD_DIM
    blk = MOBA_BLOCK
    assert s % blk == 0 and dq == kv * MOBA_GROUP * HEAD_DIM
    rows = MOBA_GROUP * blk
    return pl.pallas_call(
        _moba_prompt_kernel,
        grid=(kv, s // blk),
        in_specs=[pl.BlockSpec((blk, MOBA_GROUP * HEAD_DIM), lambda g, n: (n, g)),
                  pl.BlockSpec((s, HEAD_DIM), lambda g, n: (0, g)),
                  pl.BlockSpec((s, HEAD_DIM), lambda g, n: (0, g))],
        out_specs=pl.BlockSpec((blk, MOBA_GROUP * HEAD_DIM), lambda g, n: (n, g)),
        out_shape=jax.ShapeDtypeStruct((s, dq), F32),
        scratch_shapes=[pltpu.VMEM((s, HEAD_DIM), BF16), pltpu.VMEM((s, HEAD_DIM), BF16),
                        pltpu.VMEM((s // blk, HEAD_DIM), F32),
                        pltpu.VMEM((rows, 1), F32), pltpu.VMEM((rows, 1), F32),
                        pltpu.VMEM((rows, HEAD_DIM), F32)],
        compiler_params=_cparams(("arbitrary", "arbitrary")),
    )(q, k, v)


def _lambda_full(lq1_ref, lk1_ref, lq2_ref, lk2_ref, lam_init):
    return (jnp.exp(jnp.sum(lq1_ref[...] * lk1_ref[...], axis=1, keepdims=True))
            - jnp.exp(jnp.sum(lq2_ref[...] * lk2_ref[...], axis=1, keepdims=True)) + lam_init)


def _diff_prompt_kernel(q_ref, k_ref, v_ref, lq1_ref, lk1_ref, lq2_ref, lk2_ref, sub_ref, o_ref,
                        m1_ref, l1_ref, acc1_ref, m2_ref, l2_ref, acc2_ref, *, lam_init, tq):
    n = pl.program_id(1)
    scale = HEAD_DIM ** -0.5
    dv = 2 * HEAD_DIM
    q = q_ref[...] * scale
    q1 = _stack_heads(q, DIFF_GROUP, HEAD_DIM, 0, dv).astype(BF16)
    q2 = _stack_heads(q, DIFF_GROUP, HEAD_DIM, HEAD_DIM, dv).astype(BF16)
    for m_ref, l_ref, acc_ref in ((m1_ref, l1_ref, acc1_ref), (m2_ref, l2_ref, acc2_ref)):
        m_ref[...] = jnp.full(m_ref.shape, NEG_INF, F32)
        l_ref[...] = jnp.zeros(l_ref.shape, F32)
        acc_ref[...] = jnp.zeros(acc_ref.shape, F32)

    def step(off, mask):
        kj = k_ref[pl.ds(off, tq), :]
        vj = v_ref[pl.ds(off, tq), :]
        s1 = _qk(q1, kj[:, :HEAD_DIM])
        s2 = _qk(q2, kj[:, HEAD_DIM:])
        if mask is not None:
            s1 = jnp.where(mask, s1, NEG_INF)
            s2 = jnp.where(mask, s2, NEG_INF)
        _softmax_step(s1, vj, m1_ref, l1_ref, acc1_ref)
        _softmax_step(s2, vj, m2_ref, l2_ref, acc2_ref)

    def body(j, carry):
        step(pl.multiple_of(j * tq, tq), None)
        return carry

    lax.fori_loop(0, n, body, 0)
    step(pl.multiple_of(n * tq, tq), _causal_mask(DIFF_GROUP * tq, tq))

    lam = _lambda_full(lq1_ref, lk1_ref, lq2_ref, lk2_ref, lam_init)
    o = acc1_ref[...] / l1_ref[...] - lam * (acc2_ref[...] / l2_ref[...])
    o = _rms(o, SUBLN_EPS) * sub_ref[...] * (1.0 - lam_init)
    for h in range(DIFF_GROUP):
        o_ref[:, h * dv:(h + 1) * dv] = o[h * tq:(h + 1) * tq, :]


def _diff_prompt(q, k, v, lam_params, subln, lam_init, *, tq=256):
    s, dq = q.shape
    dv = 2 * HEAD_DIM
    kv = k.shape[1] // dv
    assert s % tq == 0 and dq == kv * DIFF_GROUP * dv
    rows = DIFF_GROUP * tq
    vec = pl.BlockSpec((1, HEAD_DIM), lambda g, n: (0, 0))
    stat = pltpu.VMEM((rows, 1), F32)
    acc = pltpu.VMEM((rows, dv), F32)
    return pl.pallas_call(
        functools.partial(_diff_prompt_kernel, lam_init=lam_init, tq=tq),
        grid=(kv, s // tq),
        in_specs=[pl.BlockSpec((tq, DIFF_GROUP * dv), lambda g, n: (n, g)),
                  pl.BlockSpec((s, dv), lambda g, n: (0, g)),
                  pl.BlockSpec((s, dv), lambda g, n: (0, g)),
                  vec, vec, vec, vec, pl.BlockSpec((1, dv), lambda g, n: (0, 0))],
        out_specs=pl.BlockSpec((tq, DIFF_GROUP * dv), lambda g, n: (n, g)),
        out_shape=jax.ShapeDtypeStruct((s, dq), F32),
        scratch_shapes=[stat, stat, acc, stat, stat, acc],
        compiler_params=_cparams(("arbitrary", "arbitrary")),
    )(q, k, v, *lam_params, subln)


def _load_heads(ref, n_heads, rows):
    return jnp.concatenate([ref[0, pl.ds(h, rows, stride=n_heads), :] for h in range(n_heads)], axis=1)


def _diag_blocks(x, width, group, n_groups):
    out = jnp.zeros((x.shape[0], width), x.dtype)
    for g in range(n_groups):
        out = jnp.where(group == g, x[:, g * width:(g + 1) * width], out)
    return out


def _split_bf16(x):
    hi = x.astype(BF16)
    return hi, (x - hi.astype(F32)).astype(BF16)


def _dot3(a, b, mm):
    return mm(a[0], b[0]) + (mm(a[1], b[0]) + mm(a[0], b[1]))


def _page_scores(q, k0, k1):
    qs = _split_bf16(q)
    return jnp.concatenate([_dot3(qs, _split_bf16(k0), _qk), _dot3(qs, _split_bf16(k1), _qk)], axis=1)


def _page_pv(p, v0, v1):
    half = p.shape[1] // 2
    mm = functools.partial(jnp.dot, preferred_element_type=F32)
    return (_dot3(_split_bf16(p[:, :half]), _split_bf16(v0), mm)
            + _dot3(_split_bf16(p[:, half:]), _split_bf16(v1), mm))


def _moba_decode_kernel(pt_ref, qbd_ref, qf_ref, knew_ref, vnew_ref, k0_ref, k1_ref, v0_ref, v1_ref, o_ref,
                        m_ref, l_ref, g_ref, oall_ref, *, nblk, kv):
    del pt_ref
    j = pl.program_id(1)
    scale = HEAD_DIM ** -0.5

    @pl.when(j == 0)
    def _():
        m_ref[...] = jnp.full(m_ref.shape, NEG_INF, F32)
        l_ref[...] = jnp.zeros(l_ref.shape, F32)
        g_ref[...] = jnp.full(g_ref.shape, NEG_INF, F32)

    qbd = qbd_ref[0]
    k0 = _load_heads(k0_ref, kv, PAGE_SIZE)
    k1 = _load_heads(k1_ref, kv, PAGE_SIZE)
    s = _page_scores(qbd * scale, k0, k1)
    ksum = jnp.sum(k0, axis=0, keepdims=True) + jnp.sum(k1, axis=0, keepdims=True)
    gate = jnp.sum(qbd * ksum, axis=1, keepdims=True) * (1.0 / MOBA_BLOCK)
    m = jnp.max(s, axis=1, keepdims=True)
    p = jnp.exp(s - m)
    pv = _page_pv(p, _load_heads(v0_ref, kv, PAGE_SIZE), _load_heads(v1_ref, kv, PAGE_SIZE))
    nh = qbd.shape[0]
    hgrp = lax.broadcasted_iota(jnp.int32, (nh, HEAD_DIM), 0) // MOBA_GROUP
    lane = lax.broadcasted_iota(jnp.int32, (nh, LANES), 1)
    m_ref[...] = jnp.where(lane == j, m, m_ref[...])
    l_ref[...] = jnp.where(lane == j, jnp.sum(p, axis=1, keepdims=True), l_ref[...])
    g_ref[...] = jnp.where(lane == j, gate, g_ref[...])
    oall_ref[j] = _diag_blocks(pv, HEAD_DIM, hgrp, kv)

    @pl.when(j == nblk - 1)
    def _():
        sel, _, _ = _topk_mask(g_ref[...], min(MOBA_TOPK, nblk), LANES)
        s_new = jnp.sum(qf_ref[0] * knew_ref[0], axis=1, keepdims=True) * scale
        m_all = m_ref[...]
        mx = jnp.maximum(jnp.max(jnp.where(sel > 0.0, m_all, NEG_INF), axis=1, keepdims=True), s_new)
        w = jnp.where(sel > 0.0, jnp.exp(m_all - mx), 0.0)
        e_new = jnp.exp(s_new - mx)
        den = jnp.sum(w * l_ref[...], axis=1, keepdims=True) + e_new
        num = e_new * vnew_ref[0]
        for b in range(nblk):
            num = num + w[:, b:b + 1] * oall_ref[b]
        o_ref[0] = num / den


def _paged_specs(npages, rows_per_page):
    def spec(which):
        return pl.BlockSpec((1, rows_per_page, HEAD_DIM), lambda b, j, pt: (pt[b * npages + 2 * j + which], 0, 0))
    return spec(0), spec(1)


def _moba_decode(page_table, qbd, qf, knew, vnew, cache_k, cache_v):
    nseq, nh, _ = qf.shape
    npages = page_table.shape[1]
    kv = cache_k.shape[1] // PAGE_SIZE
    nblk = npages * PAGE_SIZE // MOBA_BLOCK
    assert MOBA_BLOCK == 2 * PAGE_SIZE and nblk <= LANES
    p0, p1 = _paged_specs(npages, PAGE_SIZE * kv)
    per_seq = lambda w: pl.BlockSpec((1, nh, w), lambda b, j, pt: (b, 0, 0))
    gs = pltpu.PrefetchScalarGridSpec(
        num_scalar_prefetch=1, grid=(nseq, nblk),
        in_specs=[per_seq(kv * HEAD_DIM), per_seq(HEAD_DIM), per_seq(HEAD_DIM), per_seq(HEAD_DIM),
                  p0, p1, p0, p1],
        out_specs=per_seq(HEAD_DIM),
        scratch_shapes=[pltpu.VMEM((nh, LANES), F32), pltpu.VMEM((nh, LANES), F32), pltpu.VMEM((nh, LANES), F32),
                        pltpu.VMEM((nblk, nh, HEAD_DIM), F32)])
    return pl.pallas_call(
        functools.partial(_moba_decode_kernel, nblk=nblk, kv=kv),
        grid_spec=gs,
        out_shape=jax.ShapeDtypeStruct((nseq, nh, HEAD_DIM), F32),
        compiler_params=_cparams(("arbitrary", "arbitrary")),
    )(page_table.reshape(-1), qbd, qf, knew, vnew, cache_k, cache_k, cache_v, cache_v)


def _diff_decode_kernel(pt_ref, qbd_ref, qf_ref, knew_ref, vnew_ref, k0_ref, k1_ref, v0_ref, v1_ref,
                        lq1_ref, lk1_ref, lq2_ref, lk2_ref, sub_ref, o_ref, m_ref, l_ref, acc_ref,
                        *, nsteps, kv, lam_init):
    del pt_ref
    j = pl.program_id(1)
    scale = HEAD_DIM ** -0.5
    dv = 2 * HEAD_DIM

    @pl.when(j == 0)
    def _():
        m_ref[...] = jnp.full(m_ref.shape, NEG_INF, F32)
        l_ref[...] = jnp.zeros(l_ref.shape, F32)
        acc_ref[...] = jnp.zeros(acc_ref.shape, F32)

    q = qbd_ref[0] * scale
    s = _page_scores(q, _load_heads(k0_ref, 2 * kv, PAGE_SIZE), _load_heads(k1_ref, 2 * kv, PAGE_SIZE))
    v0 = jnp.concatenate([v0_ref[0, :, g, :] for g in range(kv)], axis=1)
    v1 = jnp.concatenate([v1_ref[0, :, g, :] for g in range(kv)], axis=1)
    m_prev = m_ref[...]
    m_new = jnp.maximum(m_prev, jnp.max(s, axis=1, keepdims=True))
    alpha = jnp.exp(m_prev - m_new)
    p = jnp.exp(s - m_new)
    l_ref[...] = alpha * l_ref[...] + jnp.sum(p, axis=1, keepdims=True)
    acc_ref[...] = alpha * acc_ref[...] + _page_pv(p, v0, v1)
    m_ref[...] = m_new

    @pl.when(j == nsteps - 1)
    def _():
        rows = acc_ref.shape[0]
        nh = rows // 2
        hgrp = (lax.broadcasted_iota(jnp.int32, (rows, dv), 0) & (nh - 1)) // DIFF_GROUP
        acc = _diag_blocks(acc_ref[...], dv, hgrp, kv)
        s_new = jnp.sum(qf_ref[0] * knew_ref[0], axis=1, keepdims=True) * scale
        mx = jnp.maximum(m_ref[...], s_new)
        a = jnp.exp(m_ref[...] - mx)
        e = jnp.exp(s_new - mx)
        o2 = (a * acc + e * vnew_ref[0]) / (a * l_ref[...] + e)
        lam = _lambda_full(lq1_ref, lk1_ref, lq2_ref, lk2_ref, lam_init)
        o = o2[:nh] - lam * o2[nh:]
        o_ref[0] = _rms(o, SUBLN_EPS) * sub_ref[...] * (1.0 - lam_init)


def _diff_decode(page_table, qbd, qf, knew, vnew, cache_k, cache_v, lam_params, subln, lam_init):
    nseq, rows, _ = qf.shape
    npages = page_table.shape[1]
    kv = cache_v.shape[2]
    dv = 2 * HEAD_DIM
    assert npages % 2 == 0 and rows & (rows - 1) == 0
    nsteps = npages // 2
    k0, k1 = _paged_specs(npages, PAGE_SIZE * 2 * kv)
    vspec = lambda which: pl.BlockSpec((1, PAGE_SIZE, kv, dv),
                                       lambda b, j, pt: (pt[b * npages + 2 * j + which], 0, 0, 0))
    per_seq = lambda r, w: pl.BlockSpec((1, r, w), lambda b, j, pt: (b, 0, 0))
    vec = pl.BlockSpec((1, HEAD_DIM), lambda b, j, pt: (0, 0))
    gs = pltpu.PrefetchScalarGridSpec(
        num_scalar_prefetch=1, grid=(nseq, nsteps),
        in_specs=[per_seq(rows, 2 * kv * HEAD_DIM), per_seq(rows, HEAD_DIM), per_seq(rows, HEAD_DIM),
                  per_seq(rows, dv), k0, k1, vspec(0), vspec(1), vec, vec, vec, vec,
                  pl.BlockSpec((1, dv), lambda b, j, pt: (0, 0))],
        out_specs=per_seq(rows // 2, dv),
        scratch_shapes=[pltpu.VMEM((rows, 1), F32), pltpu.VMEM((rows, 1), F32),
                        pltpu.VMEM((rows, kv * dv), F32)])
    return pl.pallas_call(
        functools.partial(_diff_decode_kernel, nsteps=nsteps, kv=kv, lam_init=lam_init),
        grid_spec=gs,
        out_shape=jax.ShapeDtypeStruct((nseq, rows // 2, dv), F32),
        compiler_params=_cparams(("arbitrary", "arbitrary")),
    )(page_table.reshape(-1), qbd, qf, knew, vnew, cache_k, cache_k, cache_v, cache_v, *lam_params, subln)


def _route_kernel(lg_ref, idx_ref, wts_ref, rank_ref, cnt_ref, *, n_tokens, tm):
    i = pl.program_id(0)

    @pl.when(i == 0)
    def _():
        cnt_ref[...] = jnp.zeros(cnt_ref.shape, F32)

    lg = lg_ref[...]
    ne = lg.shape[1]
    sel, vals, idxs = _topk_mask(lg, TOP_K, ne)
    row = i * tm + lax.broadcasted_iota(jnp.int32, (tm, 1), 0)
    sel = jnp.where(row < n_tokens, sel, 0.0)
    ex = [jnp.exp(v - vals[0]) for v in vals]
    den = functools.reduce(lambda a, b: a + b, ex)
    r = lax.broadcasted_iota(jnp.int32, (tm, tm), 0)
    c = lax.broadcasted_iota(jnp.int32, (tm, tm), 1)
    earlier = jnp.where(c < r, 1.0, 0.0).astype(BF16)
    before = jnp.dot(earlier, sel.astype(BF16), preferred_element_type=F32) + cnt_ref[...]
    lane = lax.broadcasted_iota(jnp.int32, (tm, ne), 1)
    lane_k = lax.broadcasted_iota(jnp.int32, (tm, TOP_K), 1)
    idx_o = jnp.zeros((tm, TOP_K), jnp.int32)
    rank_o = jnp.zeros((tm, TOP_K), jnp.int32)
    w_o = jnp.zeros((tm, TOP_K), F32)
    for k in range(TOP_K):
        rank = jnp.sum(jnp.where(lane == idxs[k], before, 0.0), axis=1, keepdims=True)
        idx_o = jnp.where(lane_k == k, idxs[k], idx_o)
        rank_o = jnp.where(lane_k == k, rank.astype(jnp.int32), rank_o)
        w_o = jnp.where(lane_k == k, ex[k] / den, w_o)
    idx_ref[...] = idx_o
    rank_ref[...] = rank_o
    wts_ref[...] = w_o
    cnt_ref[...] = cnt_ref[...] + jnp.sum(sel, axis=0, keepdims=True)


def _route(logits, n_tokens):
    npad, ne = logits.shape
    tm = ROUTE_TM
    out4 = pl.BlockSpec((tm, TOP_K), lambda i: (i, 0))
    return pl.pallas_call(
        functools.partial(_route_kernel, n_tokens=n_tokens, tm=tm),
        grid=(npad // tm,),
        in_specs=[pl.BlockSpec((tm, ne), lambda i: (i, 0))],
        out_specs=[out4, out4, out4, pl.BlockSpec((1, ne), lambda i: (0, 0))],
        out_shape=[jax.ShapeDtypeStruct((npad, TOP_K), jnp.int32), jax.ShapeDtypeStruct((npad, TOP_K), F32),
                   jax.ShapeDtypeStruct((npad, TOP_K), jnp.int32), jax.ShapeDtypeStruct((1, ne), F32)],
        compiler_params=_cparams(("arbitrary",)),
    )(logits)


def _scatter_kernel(pos_ref, h_ref, xin_ref, xout_ref, sem, *, n_tokens, tm):
    del xin_ref
    base = pl.program_id(0) * tm
    n_here = jnp.minimum(n_tokens - base, tm)

    def copy(r, k):
        return pltpu.make_async_copy(h_ref.at[r], xout_ref.at[pos_ref[TOP_K * (base + r) + k]], sem)

    def start(r, carry):
        for k in range(TOP_K):
            copy(r, k).start()
        return carry

    def wait(r, carry):
        for k in range(TOP_K):
            copy(r, k).wait()
        return carry

    lax.fori_loop(0, n_here, start, 0)
    lax.fori_loop(0, n_here, wait, 0)


def _scatter_rows(pos_flat, h3, n_rows, n_tokens):
    n_pad, s, l = h3.shape
    tm = COMBINE_TM
    zeros = jnp.zeros((n_rows, s, l), h3.dtype)
    anyspec = pl.BlockSpec(memory_space=pl.ANY)
    gs = pltpu.PrefetchScalarGridSpec(
        num_scalar_prefetch=1, grid=(-(-n_tokens // tm),),
        in_specs=[pl.BlockSpec((tm, s, l), lambda i, pos: (i, 0, 0)), anyspec], out_specs=anyspec,
        scratch_shapes=[pltpu.SemaphoreType.DMA(())])
    return pl.pallas_call(
        functools.partial(_scatter_kernel, n_tokens=n_tokens, tm=tm),
        grid_spec=gs,
        out_shape=jax.ShapeDtypeStruct((n_rows, s, l), h3.dtype),
        input_output_aliases={2: 0},
        compiler_params=_cparams(("arbitrary",)),
    )(pos_flat, h3, zeros)


def _swiglu_pairs(u):
    gl = jnp.minimum(u, SWIGLU_LIMIT)
    lin = jnp.clip(pltpu.roll(u, LANES - 1, 1), -SWIGLU_LIMIT, SWIGLU_LIMIT)
    even = (lax.broadcasted_iota(jnp.int32, u.shape, 1) & 1) == 0
    return jnp.where(even, gl * jax.nn.sigmoid(SWIGLU_ALPHA * gl) * (lin + 1.0), 0.0)


def _expert_kernel(gblk_ref, gexp_ref, grows_ref, x_ref, w1_ref, b1_ref, w2_ref, b2_ref, o_ref,
                   wb_ref, act_ref, *, n1):
    del gblk_ref, gexp_ref
    g = pl.program_id(0)
    s = pl.program_id(1)
    rows = grows_ref[g]
    tn = EXPERT_TN
    sub = EXPERT_SUB
    half = tn // 2
    d_in = x_ref.shape[1]
    d_ff = w2_ref.shape[1]

    @pl.when((rows > 0) & (s < n1))
    def _():
        wb_ref[0:d_in, :] = w1_ref[0].astype(BF16)
        rr = lax.broadcasted_iota(jnp.int32, (tn, half), 0)
        cc = lax.broadcasted_iota(jnp.int32, (tn, half), 1)
        pick_even = jnp.where(rr == 2 * cc, 1.0, 0.0).astype(BF16)
        for t in range(EXPERT_ROWS // sub):
            @pl.when(t * sub < rows)
            def _():
                u = jnp.dot(x_ref[t * sub:(t + 1) * sub, :], wb_ref[0:d_in, :],
                            preferred_element_type=F32) + b1_ref[0]
                a = jnp.concatenate([_swiglu_pairs(u[:, c * LANES:(c + 1) * LANES])
                                     for c in range(tn // LANES)], axis=1).astype(BF16)
                a = jnp.dot(a, pick_even, preferred_element_type=F32).astype(BF16)

                @pl.when(s % 2 == 0)
                def _():
                    act_ref[s // 2, t * sub:(t + 1) * sub, 0:half] = a

                @pl.when(s % 2 == 1)
                def _():
                    act_ref[s // 2, t * sub:(t + 1) * sub, half:tn] = a

    @pl.when((rows > 0) & (s >= n1))
    def _():
        wb_ref[0:d_ff, :] = w2_ref[0].astype(BF16)
        for t in range(EXPERT_ROWS // sub):
            @pl.when(t * sub < rows)
            def _():
                acc = jnp.zeros((sub, tn), F32) + b2_ref[0]
                for c in range(d_ff // tn):
                    acc = acc + jnp.dot(act_ref[c, t * sub:(t + 1) * sub, :], wb_ref[c * tn:(c + 1) * tn, :],
                                        preferred_element_type=F32)
                o_ref[t * sub:(t + 1) * sub, :] = acc


def _experts(gblk, gexp, grows, xs, w1, b1, w2, b2):
    n_rows, d = xs.shape
    ne, _, f2 = w1.shape
    d_ff = w2.shape[1]
    tn = EXPERT_TN
    n1, n2 = f2 // tn, d // tn
    last = n1 + n2 - 1
    assert f2 == 2 * d_ff and n1 % 2 == 0 and d_ff % tn == 0 and n_rows % EXPERT_ROWS == 0

    def step(g, s, gr):
        return jnp.where(gr[g] > 0, s, last)

    gs = pltpu.PrefetchScalarGridSpec(
        num_scalar_prefetch=3, grid=(n_rows // EXPERT_ROWS, n1 + n2),
        in_specs=[
            pl.BlockSpec((EXPERT_ROWS, d), lambda g, s, gb, ge, gr: (gb[g], 0)),
            pl.BlockSpec((1, d, tn), lambda g, s, gb, ge, gr: (ge[g], 0, jnp.minimum(step(g, s, gr), n1 - 1))),
            pl.BlockSpec((1, 1, tn), lambda g, s, gb, ge, gr: (ge[g], 0, jnp.minimum(step(g, s, gr), n1 - 1))),
            pl.BlockSpec((1, d_ff, tn), lambda g, s, gb, ge, gr: (ge[g], 0, jnp.maximum(step(g, s, gr) - n1, 0))),
            pl.BlockSpec((1, 1, tn), lambda g, s, gb, ge, gr: (ge[g], 0, jnp.maximum(step(g, s, gr) - n1, 0))),
        ],
        out_specs=pl.BlockSpec((EXPERT_ROWS, tn),
                               lambda g, s, gb, ge, gr: (gb[g], jnp.maximum(step(g, s, gr) - n1, 0))),
        scratch_shapes=[pltpu.VMEM((max(d, d_ff), tn), BF16),
                        pltpu.VMEM((n1 // 2, EXPERT_ROWS, tn), BF16)])
    return pl.pallas_call(
        functools.partial(_expert_kernel, n1=n1),
        grid_spec=gs,
        out_shape=jax.ShapeDtypeStruct((n_rows, d), F32),
        compiler_params=_cparams(("arbitrary", "arbitrary")),
    )(gblk, gexp, grows, xs, w1, b1.reshape(ne, 1, f2), w2, b2.reshape(ne, 1, d))


def _combine_kernel(pos_ref, w_ref, y_ref, o_ref, buf_ref, sem, *, tm):
    base = pl.program_id(0) * tm

    def copy(r, k):
        return pltpu.make_async_copy(y_ref.at[pos_ref[TOP_K * (base + r) + k]], buf_ref.at[k, r], sem)

    def start(r, carry):
        for k in range(TOP_K):
            copy(r, k).start()
        return carry

    def wait(r, carry):
        for k in range(TOP_K):
            copy(r, k).wait()
        return carry

    lax.fori_loop(0, tm, start, 0)
    lax.fori_loop(0, tm, wait, 0)

    def mix(r, carry):
        acc = w_ref[TOP_K * (base + r)] * buf_ref[0, r]
        for k in range(1, TOP_K):
            acc = acc + w_ref[TOP_K * (base + r) + k] * buf_ref[k, r]
        o_ref[r] = acc
        return carry

    lax.fori_loop(0, tm, mix, 0)


def _combine(pos_flat, w_flat, y3, n_pad):
    _, s, l = y3.shape
    tm = COMBINE_TM
    gs = pltpu.PrefetchScalarGridSpec(
        num_scalar_prefetch=1, grid=(n_pad // tm,),
        in_specs=[pl.BlockSpec(memory_space=pltpu.SMEM), pl.BlockSpec(memory_space=pl.ANY)],
        out_specs=pl.BlockSpec((tm, s, l), lambda i, pos: (i, 0, 0)),
        scratch_shapes=[pltpu.VMEM((TOP_K, tm, s, l), F32), pltpu.SemaphoreType.DMA(())])
    return pl.pallas_call(
        functools.partial(_combine_kernel, tm=tm),
        grid_spec=gs,
        out_shape=jax.ShapeDtypeStruct((n_pad, s, l), F32),
        compiler_params=_cparams(("arbitrary",)),
    )(pos_flat, w_flat, y3)


def _moe(h2, logits, w1, b1, w2, b2):
    n, d = h2.shape
    ne = logits.shape[1]
    n_pad = -(-n // ROUTE_TM) * ROUTE_TM
    assert ROUTE_TM % COMBINE_TM == 0 and d % LANES == 0
    idx, wts, rank, cnt = _route(jnp.pad(logits, ((0, n_pad - n), (0, 0))), n)

    cnt = cnt[0].astype(jnp.int32)
    nblk = (cnt + EXPERT_ROWS - 1) // EXPERT_ROWS
    blk_end = jnp.cumsum(nblk)
    blk_start = blk_end - nblk
    n_groups = (TOP_K * n) // EXPERT_ROWS + ne
    g = jnp.arange(n_groups, dtype=jnp.int32)
    used = g < blk_end[-1]
    gblk = jnp.where(used, g, blk_end[-1] - 1)
    gexp = jnp.minimum(jnp.searchsorted(blk_end, gblk, side="right"), ne - 1).astype(jnp.int32)
    grows = jnp.where(used, jnp.clip(cnt[gexp] - (gblk - blk_start[gexp]) * EXPERT_ROWS, 0, EXPERT_ROWS), 0)
    valid = (jnp.arange(n_pad) < n)[:, None]
    pos = jnp.where(valid, (blk_start * EXPERT_ROWS)[idx] + rank, 0).reshape(-1).astype(jnp.int32)
    w_flat = jnp.where(valid, wts, 0.0).reshape(-1)

    n_rows = n_groups * EXPERT_ROWS
    h3 = jnp.pad(h2, ((0, n_pad - n), (0, 0))).reshape(n_pad, d // LANES, LANES)
    xs3 = _scatter_rows(pos, h3, n_rows, n)
    ys = _experts(gblk.astype(jnp.int32), gexp, grows.astype(jnp.int32), xs3.reshape(n_rows, d), w1, b1, w2, b2)
    out3 = _combine(pos, w_flat, ys.reshape(n_rows, d // LANES, LANES), n_pad)
    return out3.reshape(n_pad, d)


def _rope_tables(pos):
    half = HEAD_DIM // 2
    inv_freq = 1.0 / (ROPE_THETA ** (jnp.arange(half, dtype=F32) / half))
    ang = pos.astype(F32)[:, None] * inv_freq[None, :]
    cos, sin = jnp.cos(ang), jnp.sin(ang)
    return jnp.concatenate([cos, cos], axis=1), jnp.concatenate([-sin, sin], axis=1)


def _in_proj(h, w_in, rope, d, *, tm, tn, precise=False):
    kvw = d // 4
    mm = functools.partial(_matmul, h, w_in, tm=tm, tn=tn, precise=precise)
    qa = mm(0, d, rope=rope)
    ka = mm(d, kvw, rope=rope)
    va = mm(d + kvw, kvw)
    qb = mm(d + 2 * kvw, d, rope=rope)
    kb = mm(2 * d + 2 * kvw, kvw, rope=rope)
    vb = mm(2 * d + 3 * kvw, kvw)
    gates = mm(2 * d + 4 * kvw, 2 * d)
    return qa, ka, va, qb, kb, vb, gates


def kernel(x_prompt, x_sample, c_prompt, c_sample, cache_moba_k, cache_moba_v, cache_diff_k, cache_diff_v, page_table, w_ada, b_ada, g_pre_mix, g_post_mix, g_pre_ffn, g_post_ffn, w_in, w_out, lambda_q1, lambda_k1, lambda_q2, lambda_k2, diff_subln, w_router, b_router, w_mlp1, b_mlp1, w_mlp2, b_mlp2):
    depth = w_in.shape[0]
    assert depth == 1 and x_prompt.shape[0] == 1 and x_sample.shape[1] == 1
    _, seq, d = x_prompt.shape
    nseq = x_sample.shape[0]
    n_pool = cache_moba_k.shape[1]
    past = page_table.shape[1] * PAGE_SIZE
    lam_init = 0.8 - 0.6 * math.exp(-0.3 * 0)
    row = lambda v: v[0][None, :]
    lam_params = [row(lambda_q1), row(lambda_k1), row(lambda_q2), row(lambda_k2)]
    subln = row(diff_subln)
    nh_a = d // HEAD_DIM
    kv_a = nh_a // MOBA_GROUP
    nh_b = d // (2 * HEAD_DIM)
    kv_b = nh_b // DIFF_GROUP

    c_all = jnp.concatenate([c_prompt, c_sample], axis=0)
    mod = _matmul(c_all, w_ada[0], 0, 6 * d, tm=c_all.shape[0], tn=512, bias=b_ada, silu_a=True, precise=True)
    mods_p = [mod[0:1, i * d:(i + 1) * d] for i in range(6)]
    mods_s = [mod[1:, i * d:(i + 1) * d] for i in range(6)]

    xp = x_prompt[0]
    shift_m, scale_m, gate_m, shift_f, scale_f, gate_f = mods_p
    h = _prenorm(xp, row(g_pre_mix), scale_m, shift_m, tm=256)
    rope_p = _rope_tables(jnp.arange(seq, dtype=jnp.int32))
    qa, ka, va, qb, kb, vb, gates = _in_proj(h, w_in[0], rope_p, d, tm=512, tn=512)
    o_a = _moba_prompt(qa, ka, va)
    o_b = _diff_prompt(qb, kb.astype(BF16), vb.astype(BF16), lam_params, subln, lam_init)
    z = _matmul(_merge(gates, o_a, o_b, tm=256), w_out[0], 0, d, tm=512, tn=512)
    x1_p, h2_p, lg_p = _post_mix(xp, z, gate_m, row(g_post_mix), row(g_pre_ffn), scale_f, shift_f,
                                 w_router[0], b_router, tm=256)
    gate_f_p = gate_f

    xs = x_sample[:, 0, :]
    shift_m, scale_m, gate_m, shift_f, scale_f, gate_f = mods_s
    h = _prenorm(xs, row(g_pre_mix), scale_m, shift_m, tm=nseq, dtype=F32)
    rope_s = _rope_tables(jnp.full((nseq,), past, jnp.int32))
    qa_s, ka_s, va_s, qb_s, kb_s, vb_s, gates_s = _in_proj(h, w_in[0], rope_s, d, tm=nseq, tn=512, precise=True)

    qf_a = qa_s.reshape(nseq, nh_a, HEAD_DIM)
    own_a = (jnp.arange(nh_a) // MOBA_GROUP)[:, None] == jnp.arange(kv_a)[None, :]
    qbd_a = jnp.where(own_a[None, :, :, None], qf_a[:, :, None, :], 0.0).reshape(nseq, nh_a, kv_a * HEAD_DIM)
    per_head = lambda t, width: jnp.repeat(t.reshape(nseq, -1, width), MOBA_GROUP, axis=1)
    o_a_s = _moba_decode(page_table, qbd_a, qf_a, per_head(ka_s, HEAD_DIM), per_head(va_s, HEAD_DIM),
                         cache_moba_k[0].reshape(n_pool, PAGE_SIZE * kv_a, HEAD_DIM),
                         cache_moba_v[0].reshape(n_pool, PAGE_SIZE * kv_a, HEAD_DIM)).reshape(nseq, d)

    qf_b = jnp.swapaxes(qb_s.reshape(nseq, nh_b, 2, HEAD_DIM), 1, 2).reshape(nseq, 2 * nh_b, HEAD_DIM)
    rr = jnp.arange(2 * nh_b)
    kcol = ((rr % nh_b) // DIFF_GROUP) * 2 + rr // nh_b
    own_b = kcol[:, None] == jnp.arange(2 * kv_b)[None, :]
    qbd_b = jnp.where(own_b[None, :, :, None], qf_b[:, :, None, :], 0.0).reshape(nseq, 2 * nh_b, 2 * kv_b * HEAD_DIM)
    knew_b = kb_s.reshape(nseq, 2 * kv_b, HEAD_DIM)[:, kcol, :]
    vnew_b = vb_s.reshape(nseq, kv_b, 2 * HEAD_DIM)[:, (rr % nh_b) // DIFF_GROUP, :]
    o_b_s = _diff_decode(page_table, qbd_b, qf_b, knew_b, vnew_b,
                         cache_diff_k[0].reshape(n_pool, PAGE_SIZE * 2 * kv_b, HEAD_DIM), cache_diff_v[0],
                         lam_params, subln, lam_init).reshape(nseq, d)

    z = _matmul(_merge(gates_s, o_a_s, o_b_s, tm=nseq, dtype=F32), w_out[0], 0, d, tm=nseq, tn=512, precise=True)
    x1_s, h2_s, lg_s = _post_mix(xs, z, gate_m, row(g_post_mix), row(g_pre_ffn), scale_f, shift_f,
                                 w_router[0], b_router, tm=nseq)

    moe = _moe(jnp.concatenate([h2_p, h2_s], axis=0), jnp.concatenate([lg_p, lg_s], axis=0),
               w_mlp1[0], b_mlp1[0], w_mlp2[0], b_mlp2[0])
    y_p = _post_ffn(x1_p, moe[:seq], gate_f_p, row(g_post_ffn), tm=256)
    y_s = _post_ffn(x1_s, moe[seq:seq + nseq], gate_f, row(g_post_ffn), tm=nseq)

    return (y_p[None], y_s[:, None, :],
            ka.reshape(1, 1, seq, kv_a, HEAD_DIM), va.reshape(1, 1, seq, kv_a, HEAD_DIM),
            kb.reshape(1, 1, seq, kv_b, 2, HEAD_DIM), vb.reshape(1, 1, seq, kv_b, 2 * HEAD_DIM),
            ka_s.reshape(1, nseq, 1, kv_a, HEAD_DIM), va_s.reshape(1, nseq, 1, kv_a, HEAD_DIM),
            kb_s.reshape(1, nseq, 1, kv_b, 2, HEAD_DIM), vb_s.reshape(1, nseq, 1, kv_b, 2 * HEAD_DIM))
```

```python
import functools
import math

import jax
import jax.numpy as jnp
from jax import lax
from jax.experimental import pallas as pl
from jax.experimental.pallas import tpu as pltpu

F32 = jnp.float32
BF16 = jnp.bfloat16

HEAD_DIM = 128
MOBA_BLOCK = 256
MOBA_TOPK = 3
MOBA_GROUP = 4
DIFF_GROUP = 4
PAGE_SIZE = 128
ROPE_THETA = 10000.0
TOP_K = 4
SWIGLU_ALPHA = 1.702
SWIGLU_LIMIT = 7.0
NORM_EPS = 1e-6
SUBLN_EPS = 1e-5
NEG_INF = -1e30
LANES = 128
VMEM_LIMIT = 56 * 1024 * 1024

EXPERT_ROWS = 1280
EXPERT_SUB = 1280
EXPERT_TN = 256
ROUTE_TM = 256
COMBINE_TM = 64
DMA_WINDOW = 32


def _cparams(sem, vmem=VMEM_LIMIT):
    return pltpu.CompilerParams(dimension_semantics=sem, vmem_limit_bytes=vmem)


def _rms(x, eps):
    return x * lax.rsqrt(jnp.mean(x * x, axis=-1, keepdims=True) + eps)


def _mm_kernel(a_ref, w_ref, *rest, rope, bias, silu_a, precise):
    o_ref = rest[-1] if precise else rest[-2]
    a = a_ref[...]
    if silu_a:
        a = a * jax.nn.sigmoid(a)
    if precise:
        acc = jnp.dot(a, w_ref[...], precision=lax.Precision.HIGHEST, preferred_element_type=F32)
    else:
        wb_ref = rest[-1]

        @pl.when(pl.program_id(1) == 0)
        def _():
            wb_ref[...] = w_ref[...].astype(BF16)

        acc = jnp.dot(a.astype(BF16), wb_ref[...], preferred_element_type=F32)
    if bias:
        acc = acc + rest[0][...]
    if rope:
        cos, sin = rest[0][...], rest[1][...]
        for c in range(acc.shape[1] // HEAD_DIM):
            blk = acc[:, c * HEAD_DIM:(c + 1) * HEAD_DIM]
            o_ref[:, c * HEAD_DIM:(c + 1) * HEAD_DIM] = blk * cos + pltpu.roll(blk, HEAD_DIM // 2, 1) * sin
    else:
        o_ref[...] = acc


def _matmul(a, w, col0, ncols, *, tm, tn, rope=None, bias=None, silu_a=False, precise=False):
    m, k = a.shape
    assert precise == (a.dtype == F32)
    assert m % tm == 0 and ncols % tn == 0 and col0 % tn == 0
    assert rope is None or bias is None
    cb = col0 // tn
    in_specs = [pl.BlockSpec((tm, k), lambda j, i: (i, 0)),
                pl.BlockSpec((k, tn), lambda j, i: (0, cb + j))]
    args = [a, w]
    if rope is not None:
        in_specs += [pl.BlockSpec((tm, HEAD_DIM), lambda j, i: (i, 0))] * 2
        args += list(rope)
    if bias is not None:
        in_specs.append(pl.BlockSpec((1, tn), lambda j, i: (0, cb + j)))
        args.append(bias)
    return pl.pallas_call(
        functools.partial(_mm_kernel, rope=rope is not None, bias=bias is not None, silu_a=silu_a,
                          precise=precise),
        grid=(ncols // tn, m // tm),
        in_specs=in_specs,
        out_specs=pl.BlockSpec((tm, tn), lambda j, i: (i, j)),
        out_shape=jax.ShapeDtypeStruct((m, ncols), F32),
        scratch_shapes=[] if precise else [pltpu.VMEM((k, tn), BF16)],
        compiler_params=_cparams(("arbitrary", "arbitrary")),
    )(*args)


def _prenorm_kernel(x_ref, g_ref, scale_ref, shift_ref, h_ref):
    y = _rms(x_ref[...], NORM_EPS) * g_ref[...]
    h_ref[...] = (y * (1.0 + scale_ref[...]) + shift_ref[...]).astype(h_ref.dtype)


def _row_spec(per_row, tm, d):
    return pl.BlockSpec((tm, d), lambda i: (i, 0)) if per_row else pl.BlockSpec((1, d), lambda i: (0, 0))


def _prenorm(x, g, scale, shift, *, tm, dtype=BF16):
    m, d = x.shape
    per_row = scale.shape[0] != 1
    return pl.pallas_call(
        _prenorm_kernel,
        grid=(m // tm,),
        in_specs=[pl.BlockSpec((tm, d), lambda i: (i, 0)), pl.BlockSpec((1, d), lambda i: (0, 0)),
                  _row_spec(per_row, tm, d), _row_spec(per_row, tm, d)],
        out_specs=pl.BlockSpec((tm, d), lambda i: (i, 0)),
        out_shape=jax.ShapeDtypeStruct((m, d), dtype),
        compiler_params=_cparams(("arbitrary",)),
    )(x, g, scale, shift)


def _merge_kernel(ga_ref, gb_ref, oa_ref, ob_ref, o_ref):
    o_ref[...] = (jax.nn.sigmoid(ga_ref[...]) * oa_ref[...]
                  + jax.nn.sigmoid(gb_ref[...]) * ob_ref[...]).astype(o_ref.dtype)


def _merge(gates, o_a, o_b, *, tm, dtype=BF16):
    m, d = o_a.shape
    tc = min(d, 1024)
    nc = d // tc
    return pl.pallas_call(
        _merge_kernel,
        grid=(m // tm, nc),
        in_specs=[pl.BlockSpec((tm, tc), lambda i, j: (i, j)),
                  pl.BlockSpec((tm, tc), lambda i, j: (i, nc + j)),
                  pl.BlockSpec((tm, tc), lambda i, j: (i, j)),
                  pl.BlockSpec((tm, tc), lambda i, j: (i, j))],
        out_specs=pl.BlockSpec((tm, tc), lambda i, j: (i, j)),
        out_shape=jax.ShapeDtypeStruct((m, d), dtype),
        compiler_params=_cparams(("arbitrary", "arbitrary")),
    )(gates, gates, o_a, o_b)


def _post_mix_kernel(x_ref, z_ref, gate_ref, gpost_ref, gpre_ref, scale_ref, shift_ref, wr_ref, br_ref,
                     x1_ref, h2_ref, lg_ref):
    x1 = x_ref[...] + gate_ref[...] * (_rms(z_ref[...], NORM_EPS) * gpost_ref[...])
    x1_ref[...] = x1
    h2 = _rms(x1, NORM_EPS) * gpre_ref[...] * (1.0 + scale_ref[...]) + shift_ref[...]
    h2_ref[...] = h2.astype(h2_ref.dtype)
    lg_ref[...] = jnp.dot(h2, wr_ref[...], precision=lax.Precision.HIGHEST,
                          preferred_element_type=F32) + br_ref[...]


def _post_mix(x, z, gate, g_post, g_pre, scale, shift, w_router, b_router, *, tm):
    m, d = x.shape
    per_row = gate.shape[0] != 1
    ne = w_router.shape[1]
    row = pl.BlockSpec((tm, d), lambda i: (i, 0))
    vec = pl.BlockSpec((1, d), lambda i: (0, 0))
    mod = _row_spec(per_row, tm, d)
    return pl.pallas_call(
        _post_mix_kernel,
        grid=(m // tm,),
        in_specs=[row, row, mod, vec, vec, mod, mod,
                  pl.BlockSpec((d, ne), lambda i: (0, 0)), pl.BlockSpec((1, ne), lambda i: (0, 0))],
        out_specs=[row, row, pl.BlockSpec((tm, ne), lambda i: (i, 0))],
        out_shape=[jax.ShapeDtypeStruct((m, d), F32), jax.ShapeDtypeStruct((m, d), BF16),
                   jax.ShapeDtypeStruct((m, ne), F32)],
        compiler_params=_cparams(("arbitrary",)),
    )(x, z, gate, g_post, g_pre, scale, shift, w_router, b_router)


def _post_ffn_kernel(x_ref, z_ref, gate_ref, gpost_ref, y_ref):
    y_ref[...] = x_ref[...] + gate_ref[...] * (_rms(z_ref[...], NORM_EPS) * gpost_ref[...])


def _post_ffn(x, z, gate, g_post, *, tm):
    m, d = x.shape
    per_row = gate.shape[0] != 1
    row = pl.BlockSpec((tm, d), lambda i: (i, 0))
    return pl.pallas_call(
        _post_ffn_kernel,
        grid=(m // tm,),
        in_specs=[row, row, _row_spec(per_row, tm, d), pl.BlockSpec((1, d), lambda i: (0, 0))],
        out_specs=row,
        out_shape=jax.ShapeDtypeStruct((m, d), F32),
        compiler_params=_cparams(("arbitrary",)),
    )(x, z, gate, g_post)


def _topk_mask(score, k, n_valid_lanes):
    lane = lax.broadcasted_iota(jnp.int32, score.shape, 1)
    sel = jnp.zeros(score.shape, F32)
    vals, idxs = [], []
    for _ in range(k):
        m = jnp.max(score, axis=1, keepdims=True)
        idx = jnp.min(jnp.where(score == m, lane, n_valid_lanes), axis=1, keepdims=True)
        pick = lane == idx
        sel = jnp.where(pick & (m > 0.5 * NEG_INF), 1.0, sel)
        score = jnp.where(pick, -jnp.inf, score)
        vals.append(m)
        idxs.append(idx)
    return sel, vals, idxs


def _stack_heads(x, n, width, offset=0, stride=None):
    stride = width if stride is None else stride
    return jnp.concatenate([x[:, h * stride + offset:h * stride + offset + width] for h in range(n)], axis=0)


def _softmax_step(s, v, m_ref, l_ref, acc_ref):
    m_prev = m_ref[...]
    m_new = jnp.maximum(m_prev, jnp.max(s, axis=1, keepdims=True))
    alpha = jnp.exp(m_prev - m_new)
    p = jnp.exp(s - m_new)
    l_ref[...] = alpha * l_ref[...] + jnp.sum(p, axis=1, keepdims=True)
    acc_ref[...] = alpha * acc_ref[...] + jnp.dot(p.astype(BF16), v, preferred_element_type=F32)
    m_ref[...] = m_new


def _causal_mask(rows, blk):
    r = lax.broadcasted_iota(jnp.int32, (rows, blk), 0) & (blk - 1)
    c = lax.broadcasted_iota(jnp.int32, (rows, blk), 1)
    return c <= r


def _qk(q, k):
    return lax.dot_general(q, k, (((1,), (1,)), ((), ())), preferred_element_type=F32)


def _moba_prompt_kernel(q_ref, k_ref, v_ref, o_ref, kb_ref, vb_ref, kmean_ref, m_ref, l_ref, acc_ref):
    n = pl.program_id(1)
    blk = MOBA_BLOCK
    nb = k_ref.shape[0] // blk
    scale = HEAD_DIM ** -0.5

    @pl.when(n == 0)
    def _():
        kb_ref[...] = k_ref[...].astype(BF16)
        vb_ref[...] = v_ref[...].astype(BF16)
        for b in range(nb):
            kmean_ref[b:b + 1, :] = jnp.mean(k_ref[b * blk:(b + 1) * blk, :], axis=0, keepdims=True)

    q4 = _stack_heads(q_ref[...], MOBA_GROUP, HEAD_DIM)
    gate = lax.dot_general(q4, kmean_ref[...], (((1,), (1,)), ((), ())),
                           precision=lax.Precision.HIGHEST, preferred_element_type=F32)
    lane = lax.broadcasted_iota(jnp.int32, gate.shape, 1)
    gate = jnp.where(lane < n, gate, NEG_INF)
    sel, _, _ = _topk_mask(gate, min(MOBA_TOPK, nb - 1), nb)

    qb = (q4 * scale).astype(BF16)
    own = pl.multiple_of(n * blk, blk)
    s = _qk(qb, kb_ref[pl.ds(own, blk), :])
    s = jnp.where(_causal_mask(MOBA_GROUP * blk, blk), s, NEG_INF)
    m0 = jnp.max(s, axis=1, keepdims=True)
    p = jnp.exp(s - m0)
    m_ref[...] = m0
    l_ref[...] = jnp.sum(p, axis=1, keepdims=True)
    acc_ref[...] = jnp.dot(p.astype(BF16), vb_ref[pl.ds(own, blk), :], preferred_element_type=F32)

    def body(j, carry):
        picked = jnp.sum(jnp.where(lane == j, sel, 0.0), axis=1, keepdims=True)
        off = pl.multiple_of(j * blk, blk)
        sj = _qk(qb, kb_ref[pl.ds(off, blk), :])
        sj = jnp.where(picked > 0.0, sj, NEG_INF)
        _softmax_step(sj, vb_ref[pl.ds(off, blk), :], m_ref, l_ref, acc_ref)
        return carry

    lax.fori_loop(0, n, body, 0)
    out = acc_ref[...] / l_ref[...]
    for h in range(MOBA_GROUP):
        o_ref[:, h * HEAD_DIM:(h + 1) * HEAD_DIM] = out[h * blk:(h + 1) * blk, :]


def _moba_prompt(q, k, v):
    s, dq = q.shape
    kv = k.shape[1] // HEAD_DIM
    blk = MOBA_BLOCK
    assert s % blk == 0 and dq == kv * MOBA_GROUP * HEAD_DIM
    rows = MOBA_GROUP * blk
    return pl.pallas_call(
        _moba_prompt_kernel,
        grid=(kv, s // blk),
        in_specs=[pl.BlockSpec((blk, MOBA_GROUP * HEAD_DIM), lambda g, n: (n, g)),
                  pl.BlockSpec((s, HEAD_DIM), lambda g, n: (0, g)),
                  pl.BlockSpec((s, HEAD_DIM), lambda g, n: (0, g))],
        out_specs=pl.BlockSpec((blk, MOBA_GROUP * HEAD_DIM), lambda g, n: (n, g)),
        out_shape=jax.ShapeDtypeStruct((s, dq), F32),
        scratch_shapes=[pltpu.VMEM((s, HEAD_DIM), BF16), pltpu.VMEM((s, HEAD_DIM), BF16),
                        pltpu.VMEM((s // blk, HEAD_DIM), F32),
                        pltpu.VMEM((rows, 1), F32), pltpu.VMEM((rows, 1), F32),
                        pltpu.VMEM((rows, HEAD_DIM), F32)],
        compiler_params=_cparams(("arbitrary", "arbitrary")),
    )(q, k, v)


def _lambda_full(lq1_ref, lk1_ref, lq2_ref, lk2_ref, lam_init):
    return (jnp.exp(jnp.sum(lq1_ref[...] * lk1_ref[...], axis=1, keepdims=True))
            - jnp.exp(jnp.sum(lq2_ref[...] * lk2_ref[...], axis=1, keepdims=True)) + lam_init)


def _diff_prompt_kernel(q_ref, k_ref, v_ref, lq1_ref, lk1_ref, lq2_ref, lk2_ref, sub_ref, o_ref,
                        m1_ref, l1_ref, acc1_ref, m2_ref, l2_ref, acc2_ref, *, lam_init, tq):
    n = pl.program_id(1)
    scale = HEAD_DIM ** -0.5
    dv = 2 * HEAD_DIM
    q = q_ref[...] * scale
    q1 = _stack_heads(q, DIFF_GROUP, HEAD_DIM, 0, dv).astype(BF16)
    q2 = _stack_heads(q, DIFF_GROUP, HEAD_DIM, HEAD_DIM, dv).astype(BF16)
    for m_ref, l_ref, acc_ref in ((m1_ref, l1_ref, acc1_ref), (m2_ref, l2_ref, acc2_ref)):
        m_ref[...] = jnp.full(m_ref.shape, NEG_INF, F32)
        l_ref[...] = jnp.zeros(l_ref.shape, F32)
        acc_ref[...] = jnp.zeros(acc_ref.shape, F32)

    def step(off, mask):
        kj = k_ref[pl.ds(off, tq), :]
        vj = v_ref[pl.ds(off, tq), :]
        s1 = _qk(q1, kj[:, :HEAD_DIM])
        s2 = _qk(q2, kj[:, HEAD_DIM:])
        if mask is not None:
            s1 = jnp.where(mask, s1, NEG_INF)
            s2 = jnp.where(mask, s2, NEG_INF)
        _softmax_step(s1, vj, m1_ref, l1_ref, acc1_ref)
        _softmax_step(s2, vj, m2_ref, l2_ref, acc2_ref)

    def body(j, carry):
        step(pl.multiple_of(j * tq, tq), None)
        return carry

    lax.fori_loop(0, n, body, 0)
    step(pl.multiple_of(n * tq, tq), _causal_mask(DIFF_GROUP * tq, tq))

    lam = _lambda_full(lq1_ref, lk1_ref, lq2_ref, lk2_ref, lam_init)
    o = acc1_ref[...] / l1_ref[...] - lam * (acc2_ref[...] / l2_ref[...])
    o = _rms(o, SUBLN_EPS) * sub_ref[...] * (1.0 - lam_init)
    for h in range(DIFF_GROUP):
        o_ref[:, h * dv:(h + 1) * dv] = o[h * tq:(h + 1) * tq, :]


def _diff_prompt(q, k, v, lam_params, subln, lam_init, *, tq=256):
    s, dq = q.shape
    dv = 2 * HEAD_DIM
    kv = k.shape[1] // dv
    assert s % tq == 0 and dq == kv * DIFF_GROUP * dv
    rows = DIFF_GROUP * tq
    vec = pl.BlockSpec((1, HEAD_DIM), lambda g, n: (0, 0))
    stat = pltpu.VMEM((rows, 1), F32)
    acc = pltpu.VMEM((rows, dv), F32)
    return pl.pallas_call(
        functools.partial(_diff_prompt_kernel, lam_init=lam_init, tq=tq),
        grid=(kv, s // tq),
        in_specs=[pl.BlockSpec((tq, DIFF_GROUP * dv), lambda g, n: (n, g)),
                  pl.BlockSpec((s, dv), lambda g, n: (0, g)),
                  pl.BlockSpec((s, dv), lambda g, n: (0, g)),
                  vec, vec, vec, vec, pl.BlockSpec((1, dv), lambda g, n: (0, 0))],
        out_specs=pl.BlockSpec((tq, DIFF_GROUP * dv), lambda g, n: (n, g)),
        out_shape=jax.ShapeDtypeStruct((s, dq), F32),
        scratch_shapes=[stat, stat, acc, stat, stat, acc],
        compiler_params=_cparams(("arbitrary", "arbitrary")),
    )(q, k, v, *lam_params, subln)


def _load_heads(ref, n_heads, rows):
    return jnp.concatenate([ref[0, pl.ds(h, rows, stride=n_heads), :] for h in range(n_heads)], axis=1)


def _diag_blocks(x, width, group, n_groups):
    out = jnp.zeros((x.shape[0], width), x.dtype)
    for g in range(n_groups):
        out = jnp.where(group == g, x[:, g * width:(g + 1) * width], out)
    return out


def _split_bf16(x):
    hi = x.astype(BF16)
    return hi, (x - hi.astype(F32)).astype(BF16)


def _dot3(a, b, mm):
    return mm(a[0], b[0]) + (mm(a[1], b[0]) + mm(a[0], b[1]))


def _page_scores(q, k0, k1):
    qs = _split_bf16(q)
    return jnp.concatenate([_dot3(qs, _split_bf16(k0), _qk), _dot3(qs, _split_bf16(k1), _qk)], axis=1)


def _page_pv(p, v0, v1):
    half = p.shape[1] // 2
    mm = functools.partial(jnp.dot, preferred_element_type=F32)
    return (_dot3(_split_bf16(p[:, :half]), _split_bf16(v0), mm)
            + _dot3(_split_bf16(p[:, half:]), _split_bf16(v1), mm))


def _moba_decode_kernel(pt_ref, qbd_ref, qf_ref, knew_ref, vnew_ref, k0_ref, k1_ref, v0_ref, v1_ref, o_ref,
                        m_ref, l_ref, g_ref, oall_ref, *, nblk, kv):
    del pt_ref
    j = pl.program_id(1)
    scale = HEAD_DIM ** -0.5

    @pl.when(j == 0)
    def _():
        m_ref[...] = jnp.full(m_ref.shape, NEG_INF, F32)
        l_ref[...] = jnp.zeros(l_ref.shape, F32)
        g_ref[...] = jnp.full(g_ref.shape, NEG_INF, F32)

    qbd = qbd_ref[0]
    k0 = _load_heads(k0_ref, kv, PAGE_SIZE)
    k1 = _load_heads(k1_ref, kv, PAGE_SIZE)
    s = _page_scores(qbd * scale, k0, k1)
    ksum = jnp.sum(k0, axis=0, keepdims=True) + jnp.sum(k1, axis=0, keepdims=True)
    gate = jnp.sum(qbd * ksum, axis=1, keepdims=True) * (1.0 / MOBA_BLOCK)
    m = jnp.max(s, axis=1, keepdims=True)
    p = jnp.exp(s - m)
    pv = _page_pv(p, _load_heads(v0_ref, kv, PAGE_SIZE), _load_heads(v1_ref, kv, PAGE_SIZE))
    nh = qbd.shape[0]
    hgrp = lax.broadcasted_iota(jnp.int32, (nh, HEAD_DIM), 0) // MOBA_GROUP
    lane = lax.broadcasted_iota(jnp.int32, (nh, LANES), 1)
    m_ref[...] = jnp.where(lane == j, m, m_ref[...])
    l_ref[...] = jnp.where(lane == j, jnp.sum(p, axis=1, keepdims=True), l_ref[...])
    g_ref[...] = jnp.where(lane == j, gate, g_ref[...])
    oall_ref[j] = _diag_blocks(pv, HEAD_DIM, hgrp, kv)

    @pl.when(j == nblk - 1)
    def _():
        sel, _, _ = _topk_mask(g_ref[...], min(MOBA_TOPK, nblk), LANES)
        s_new = jnp.sum(qf_ref[0] * knew_ref[0], axis=1, keepdims=True) * scale
        m_all = m_ref[...]
        mx = jnp.maximum(jnp.max(jnp.where(sel > 0.0, m_all, NEG_INF), axis=1, keepdims=True), s_new)
        w = jnp.where(sel > 0.0, jnp.exp(m_all - mx), 0.0)
        e_new = jnp.exp(s_new - mx)
        den = jnp.sum(w * l_ref[...], axis=1, keepdims=True) + e_new
        num = e_new * vnew_ref[0]
        for b in range(nblk):
            num = num + w[:, b:b + 1] * oall_ref[b]
        o_ref[0] = num / den


def _paged_specs(npages, rows_per_page):
    def spec(which):
        return pl.BlockSpec((1, rows_per_page, HEAD_DIM), lambda b, j, pt: (pt[b * npages + 2 * j + which], 0, 0))
    return spec(0), spec(1)


def _moba_decode(page_table, qbd, qf, knew, vnew, cache_k, cache_v):
    nseq, nh, _ = qf.shape
    npages = page_table.shape[1]
    kv = cache_k.shape[1] // PAGE_SIZE
    nblk = npages * PAGE_SIZE // MOBA_BLOCK
    assert MOBA_BLOCK == 2 * PAGE_SIZE and nblk <= LANES
    p0, p1 = _paged_specs(npages, PAGE_SIZE * kv)
    per_seq = lambda w: pl.BlockSpec((1, nh, w), lambda b, j, pt: (b, 0, 0))
    gs = pltpu.PrefetchScalarGridSpec(
        num_scalar_prefetch=1, grid=(nseq, nblk),
        in_specs=[per_seq(kv * HEAD_DIM), per_seq(HEAD_DIM), per_seq(HEAD_DIM), per_seq(HEAD_DIM),
                  p0, p1, p0, p1],
        out_specs=per_seq(HEAD_DIM),
        scratch_shapes=[pltpu.VMEM((nh, LANES), F32), pltpu.VMEM((nh, LANES), F32), pltpu.VMEM((nh, LANES), F32),
                        pltpu.VMEM((nblk, nh, HEAD_DIM), F32)])
    return pl.pallas_call(
        functools.partial(_moba_decode_kernel, nblk=nblk, kv=kv),
        grid_spec=gs,
        out_shape=jax.ShapeDtypeStruct((nseq, nh, HEAD_DIM), F32),
        compiler_params=_cparams(("arbitrary", "arbitrary")),
    )(page_table.reshape(-1), qbd, qf, knew, vnew, cache_k, cache_k, cache_v, cache_v)


def _diff_decode_kernel(pt_ref, qbd_ref, qf_ref, knew_ref, vnew_ref, k0_ref, k1_ref, v0_ref, v1_ref,
                        lq1_ref, lk1_ref, lq2_ref, lk2_ref, sub_ref, o_ref, m_ref, l_ref, acc_ref,
                        *, nsteps, kv, lam_init):
    del pt_ref
    j = pl.program_id(1)
    scale = HEAD_DIM ** -0.5
    dv = 2 * HEAD_DIM

    @pl.when(j == 0)
    def _():
        m_ref[...] = jnp.full(m_ref.shape, NEG_INF, F32)
        l_ref[...] = jnp.zeros(l_ref.shape, F32)
        acc_ref[...] = jnp.zeros(acc_ref.shape, F32)

    q = qbd_ref[0] * scale
    s = _page_scores(q, _load_heads(k0_ref, 2 * kv, PAGE_SIZE), _load_heads(k1_ref, 2 * kv, PAGE_SIZE))
    v0 = jnp.concatenate([v0_ref[0, :, g, :] for g in range(kv)], axis=1)
    v1 = jnp.concatenate([v1_ref[0, :, g, :] for g in range(kv)], axis=1)
    m_prev = m_ref[...]
    m_new = jnp.maximum(m_prev, jnp.max(s, axis=1, keepdims=True))
    alpha = jnp.exp(m_prev - m_new)
    p = jnp.exp(s - m_new)
    l_ref[...] = alpha * l_ref[...] + jnp.sum(p, axis=1, keepdims=True)
    acc_ref[...] = alpha * acc_ref[...] + _page_pv(p, v0, v1)
    m_ref[...] = m_new

    @pl.when(j == nsteps - 1)
    def _():
        rows = acc_ref.shape[0]
        nh = rows // 2
        hgrp = (lax.broadcasted_iota(jnp.int32, (rows, dv), 0) & (nh - 1)) // DIFF_GROUP
        acc = _diag_blocks(acc_ref[...], dv, hgrp, kv)
        s_new = jnp.sum(qf_ref[0] * knew_ref[0], axis=1, keepdims=True) * scale
        mx = jnp.maximum(m_ref[...], s_new)
        a = jnp.exp(m_ref[...] - mx)
        e = jnp.exp(s_new - mx)
        o2 = (a * acc + e * vnew_ref[0]) / (a * l_ref[...] + e)
        lam = _lambda_full(lq1_ref, lk1_ref, lq2_ref, lk2_ref, lam_init)
        o = o2[:nh] - lam * o2[nh:]
        o_ref[0] = _rms(o, SUBLN_EPS) * sub_ref[...] * (1.0 - lam_init)


def _diff_decode(page_table, qbd, qf, knew, vnew, cache_k, cache_v, lam_params, subln, lam_init):
    nseq, rows, _ = qf.shape
    npages = page_table.shape[1]
    kv = cache_v.shape[2]
    dv = 2 * HEAD_DIM
    assert npages % 2 == 0 and rows & (rows - 1) == 0
    nsteps = npages // 2
    k0, k1 = _paged_specs(npages, PAGE_SIZE * 2 * kv)
    vspec = lambda which: pl.BlockSpec((1, PAGE_SIZE, kv, dv),
                                       lambda b, j, pt: (pt[b * npages + 2 * j + which], 0, 0, 0))
    per_seq = lambda r, w: pl.BlockSpec((1, r, w), lambda b, j, pt: (b, 0, 0))
    vec = pl.BlockSpec((1, HEAD_DIM), lambda b, j, pt: (0, 0))
    gs = pltpu.PrefetchScalarGridSpec(
        num_scalar_prefetch=1, grid=(nseq, nsteps),
        in_specs=[per_seq(rows, 2 * kv * HEAD_DIM), per_seq(rows, HEAD_DIM), per_seq(rows, HEAD_DIM),
                  per_seq(rows, dv), k0, k1, vspec(0), vspec(1), vec, vec, vec, vec,
                  pl.BlockSpec((1, dv), lambda b, j, pt: (0, 0))],
        out_specs=per_seq(rows // 2, dv),
        scratch_shapes=[pltpu.VMEM((rows, 1), F32), pltpu.VMEM((rows, 1), F32),
                        pltpu.VMEM((rows, kv * dv), F32)])
    return pl.pallas_call(
        functools.partial(_diff_decode_kernel, nsteps=nsteps, kv=kv, lam_init=lam_init),
        grid_spec=gs,
        out_shape=jax.ShapeDtypeStruct((nseq, rows // 2, dv), F32),
        compiler_params=_cparams(("arbitrary", "arbitrary")),
    )(page_table.reshape(-1), qbd, qf, knew, vnew, cache_k, cache_k, cache_v, cache_v, *lam_params, subln)


def _route_kernel(lg_ref, idx_ref, wts_ref, rank_ref, cnt_ref, *, n_tokens, tm):
    i = pl.program_id(0)

    @pl.when(i == 0)
    def _():
        cnt_ref[...] = jnp.zeros(cnt_ref.shape, F32)

    lg = lg_ref[...]
    ne = lg.shape[1]
    sel, vals, idxs = _topk_mask(lg, TOP_K, ne)
    row = i * tm + lax.broadcasted_iota(jnp.int32, (tm, 1), 0)
    sel = jnp.where(row < n_tokens, sel, 0.0)
    ex = [jnp.exp(v - vals[0]) for v in vals]
    den = functools.reduce(lambda a, b: a + b, ex)
    r = lax.broadcasted_iota(jnp.int32, (tm, tm), 0)
    c = lax.broadcasted_iota(jnp.int32, (tm, tm), 1)
    earlier = jnp.where(c < r, 1.0, 0.0).astype(BF16)
    before = jnp.dot(earlier, sel.astype(BF16), preferred_element_type=F32) + cnt_ref[...]
    lane = lax.broadcasted_iota(jnp.int32, (tm, ne), 1)
    lane_k = lax.broadcasted_iota(jnp.int32, (tm, TOP_K), 1)
    idx_o = jnp.zeros((tm, TOP_K), jnp.int32)
    rank_o = jnp.zeros((tm, TOP_K), jnp.int32)
    w_o = jnp.zeros((tm, TOP_K), F32)
    for k in range(TOP_K):
        rank = jnp.sum(jnp.where(lane == idxs[k], before, 0.0), axis=1, keepdims=True)
        idx_o = jnp.where(lane_k == k, idxs[k], idx_o)
        rank_o = jnp.where(lane_k == k, rank.astype(jnp.int32), rank_o)
        w_o = jnp.where(lane_k == k, ex[k] / den, w_o)
    idx_ref[...] = idx_o
    rank_ref[...] = rank_o
    wts_ref[...] = w_o
    cnt_ref[...] = cnt_ref[...] + jnp.sum(sel, axis=0, keepdims=True)


def _route(logits, n_tokens):
    npad, ne = logits.shape
    tm = ROUTE_TM
    out4 = pl.BlockSpec((tm, TOP_K), lambda i: (i, 0))
    return pl.pallas_call(
        functools.partial(_route_kernel, n_tokens=n_tokens, tm=tm),
        grid=(npad // tm,),
        in_specs=[pl.BlockSpec((tm, ne), lambda i: (i, 0))],
        out_specs=[out4, out4, out4, pl.BlockSpec((1, ne), lambda i: (0, 0))],
        out_shape=[jax.ShapeDtypeStruct((npad, TOP_K), jnp.int32), jax.ShapeDtypeStruct((npad, TOP_K), F32),
                   jax.ShapeDtypeStruct((npad, TOP_K), jnp.int32), jax.ShapeDtypeStruct((1, ne), F32)],
        compiler_params=_cparams(("arbitrary",)),
    )(logits)


def _scatter_kernel(pos_ref, h_ref, xin_ref, xout_ref, sem, *, n_tokens, tm):
    del xin_ref
    base = pl.program_id(0) * tm
    n_here = jnp.minimum(n_tokens - base, tm)

    def copy(r, k):
        return pltpu.make_async_copy(h_ref.at[r], xout_ref.at[pos_ref[TOP_K * (base + r) + k]], sem)

    def start(r, carry):
        for k in range(TOP_K):
            copy(r, k).start()
        return carry

    def wait(r, carry):
        for k in range(TOP_K):
            copy(r, k).wait()
        return carry

    lax.fori_loop(0, n_here, start, 0)
    lax.fori_loop(0, n_here, wait, 0)


def _scatter_rows(pos_flat, h3, n_rows, n_tokens):
    n_pad, s, l = h3.shape
    tm = COMBINE_TM
    zeros = jnp.zeros((n_rows, s, l), h3.dtype)
    anyspec = pl.BlockSpec(memory_space=pl.ANY)
    gs = pltpu.PrefetchScalarGridSpec(
        num_scalar_prefetch=1, grid=(-(-n_tokens // tm),),
        in_specs=[pl.BlockSpec((tm, s, l), lambda i, pos: (i, 0, 0)), anyspec], out_specs=anyspec,
        scratch_shapes=[pltpu.SemaphoreType.DMA(())])
    return pl.pallas_call(
        functools.partial(_scatter_kernel, n_tokens=n_tokens, tm=tm),
        grid_spec=gs,
        out_shape=jax.ShapeDtypeStruct((n_rows, s, l), h3.dtype),
        input_output_aliases={2: 0},
        compiler_params=_cparams(("arbitrary",)),
    )(pos_flat, h3, zeros)


def _swiglu_pairs(u):
    gl = jnp.minimum(u, SWIGLU_LIMIT)
    lin = jnp.clip(pltpu.roll(u, LANES - 1, 1), -SWIGLU_LIMIT, SWIGLU_LIMIT)
    even = (lax.broadcasted_iota(jnp.int32, u.shape, 1) & 1) == 0
    return jnp.where(even, gl * jax.nn.sigmoid(SWIGLU_ALPHA * gl) * (lin + 1.0), 0.0)


def _expert_kernel(gblk_ref, gexp_ref, grows_ref, x_ref, w1_ref, b1_ref, w2_ref, b2_ref, o_ref,
                   wb_ref, act_ref, *, n1):
    del gblk_ref, gexp_ref
    g = pl.program_id(0)
    s = pl.program_id(1)
    rows = grows_ref[g]
    tn = EXPERT_TN
    sub = EXPERT_SUB
    half = tn // 2
    d_in = x_ref.shape[1]
    d_ff = w2_ref.shape[1]

    @pl.when((rows > 0) & (s < n1))
    def _():
        wb_ref[0:d_in, :] = w1_ref[0].astype(BF16)
        rr = lax.broadcasted_iota(jnp.int32, (tn, half), 0)
        cc = lax.broadcasted_iota(jnp.int32, (tn, half), 1)
        pick_even = jnp.where(rr == 2 * cc, 1.0, 0.0).astype(BF16)
        for t in range(EXPERT_ROWS // sub):
            @pl.when(t * sub < rows)
            def _():
                u = jnp.dot(x_ref[t * sub:(t + 1) * sub, :], wb_ref[0:d_in, :],
                            preferred_element_type=F32) + b1_ref[0]
                a = jnp.concatenate([_swiglu_pairs(u[:, c * LANES:(c + 1) * LANES])
                                     for c in range(tn // LANES)], axis=1).astype(BF16)
                a = jnp.dot(a, pick_even, preferred_element_type=F32).astype(BF16)

                @pl.when(s % 2 == 0)
                def _():
                    act_ref[s // 2, t * sub:(t + 1) * sub, 0:half] = a

                @pl.when(s % 2 == 1)
                def _():
                    act_ref[s // 2, t * sub:(t + 1) * sub, half:tn] = a

    @pl.when((rows > 0) & (s >= n1))
    def _():
        wb_ref[0:d_ff, :] = w2_ref[0].astype(BF16)
        for t in range(EXPERT_ROWS // sub):
            @pl.when(t * sub < rows)
            def _():
                acc = jnp.zeros((sub, tn), F32) + b2_ref[0]
                for c in range(d_ff // tn):
                    acc = acc + jnp.dot(act_ref[c, t * sub:(t + 1) * sub, :], wb_ref[c * tn:(c + 1) * tn, :],
                                        preferred_element_type=F32)
                o_ref[t * sub:(t + 1) * sub, :] = acc


def _experts(gblk, gexp, grows, xs, w1, b1, w2, b2):
    n_rows, d = xs.shape
    ne, _, f2 = w1.shape
    d_ff = w2.shape[1]
    tn = EXPERT_TN
    n1, n2 = f2 // tn, d // tn
    last = n1 + n2 - 1
    assert f2 == 2 * d_ff and n1 % 2 == 0 and d_ff % tn == 0 and n_rows % EXPERT_ROWS == 0

    def step(g, s, gr):
        return jnp.where(gr[g] > 0, s, last)

    gs = pltpu.PrefetchScalarGridSpec(
        num_scalar_prefetch=3, grid=(n_rows // EXPERT_ROWS, n1 + n2),
        in_specs=[
            pl.BlockSpec((EXPERT_ROWS, d), lambda g, s, gb, ge, gr: (gb[g], 0)),
            pl.BlockSpec((1, d, tn), lambda g, s, gb, ge, gr: (ge[g], 0, jnp.minimum(step(g, s, gr), n1 - 1))),
            pl.BlockSpec((1, 1, tn), lambda g, s, gb, ge, gr: (ge[g], 0, jnp.minimum(step(g, s, gr), n1 - 1))),
            pl.BlockSpec((1, d_ff, tn), lambda g, s, gb, ge, gr: (ge[g], 0, jnp.maximum(step(g, s, gr) - n1, 0))),
            pl.BlockSpec((1, 1, tn), lambda g, s, gb, ge, gr: (ge[g], 0, jnp.maximum(step(g, s, gr) - n1, 0))),
        ],
        out_specs=pl.BlockSpec((EXPERT_ROWS, tn),
                               lambda g, s, gb, ge, gr: (gb[g], jnp.maximum(step(g, s, gr) - n1, 0))),
        scratch_shapes=[pltpu.VMEM((max(d, d_ff), tn), BF16),
                        pltpu.VMEM((n1 // 2, EXPERT_ROWS, tn), BF16)])
    return pl.pallas_call(
        functools.partial(_expert_kernel, n1=n1),
        grid_spec=gs,
        out_shape=jax.ShapeDtypeStruct((n_rows, d), F32),
        compiler_params=_cparams(("arbitrary", "arbitrary")),
    )(gblk, gexp, grows, xs, w1, b1.reshape(ne, 1, f2), w2, b2.reshape(ne, 1, d))


def _combine_kernel(pos_ref, w_ref, y_ref, o_ref, buf_ref, sem, *, tm):
    base = pl.program_id(0) * tm

    def copy(r, k):
        return pltpu.make_async_copy(y_ref.at[pos_ref[TOP_K * (base + r) + k]], buf_ref.at[k, r], sem)

    def start(r, carry):
        for k in range(TOP_K):
            copy(r, k).start()
        return carry

    def wait(r, carry):
        for k in range(TOP_K):
            copy(r, k).wait()
        return carry

    lax.fori_loop(0, tm, start, 0)
    lax.fori_loop(0, tm, wait, 0)

    def mix(r, carry):
        acc = w_ref[TOP_K * (base + r)] * buf_ref[0, r]
        for k in range(1, TOP_K):
            acc = acc + w_ref[TOP_K * (base + r) + k] * buf_ref[k, r]
        o_ref[r] = acc
        return carry

    lax.fori_loop(0, tm, mix, 0)


def _combine(pos_flat, w_flat, y3, n_pad):
    _, s, l = y3.shape
    tm = COMBINE_TM
    gs = pltpu.PrefetchScalarGridSpec(
        num_scalar_prefetch=1, grid=(n_pad // tm,),
        in_specs=[pl.BlockSpec(memory_space=pltpu.SMEM), pl.BlockSpec(memory_space=pl.ANY)],
        out_specs=pl.BlockSpec((tm, s, l), lambda i, pos: (i, 0, 0)),
        scratch_shapes=[pltpu.VMEM((TOP_K, tm, s, l), F32), pltpu.SemaphoreType.DMA(())])
    return pl.pallas_call(
        functools.partial(_combine_kernel, tm=tm),
        grid_spec=gs,
        out_shape=jax.ShapeDtypeStruct((n_pad, s, l), F32),
        compiler_params=_cparams(("arbitrary",)),
    )(pos_flat, w_flat, y3)


def _moe(h2, logits, w1, b1, w2, b2):
    n, d = h2.shape
    ne = logits.shape[1]
    n_pad = -(-n // ROUTE_TM) * ROUTE_TM
    assert ROUTE_TM % COMBINE_TM == 0 and d % LANES == 0
    idx, wts, rank, cnt = _route(jnp.pad(logits, ((0, n_pad - n), (0, 0))), n)

    cnt = cnt[0].astype(jnp.int32)
    nblk = (cnt + EXPERT_ROWS - 1) // EXPERT_ROWS
    blk_end = jnp.cumsum(nblk)
    blk_start = blk_end - nblk
    n_groups = (TOP_K * n) // EXPERT_ROWS + ne
    g = jnp.arange(n_groups, dtype=jnp.int32)
    used = g < blk_end[-1]
    gblk = jnp.where(used, g, blk_end[-1] - 1)
    gexp = jnp.minimum(jnp.searchsorted(blk_end, gblk, side="right"), ne - 1).astype(jnp.int32)
    grows = jnp.where(used, jnp.clip(cnt[gexp] - (gblk - blk_start[gexp]) * EXPERT_ROWS, 0, EXPERT_ROWS), 0)
    valid = (jnp.arange(n_pad) < n)[:, None]
    pos = jnp.where(valid, (blk_start * EXPERT_ROWS)[idx] + rank, 0).reshape(-1).astype(jnp.int32)
    w_flat = jnp.where(valid, wts, 0.0).reshape(-1)

    n_rows = n_groups * EXPERT_ROWS
    h3 = jnp.pad(h2, ((0, n_pad - n), (0, 0))).reshape(n_pad, d // LANES, LANES)
    xs3 = _scatter_rows(pos, h3, n_rows, n)
    ys = _experts(gblk.astype(jnp.int32), gexp, grows.astype(jnp.int32), xs3.reshape(n_rows, d), w1, b1, w2, b2)
    out3 = _combine(pos, w_flat, ys.reshape(n_rows, d // LANES, LANES), n_pad)
    return out3.reshape(n_pad, d)


def _rope_tables(pos):
    half = HEAD_DIM // 2
    inv_freq = 1.0 / (ROPE_THETA ** (jnp.arange(half, dtype=F32) / half))
    ang = pos.astype(F32)[:, None] * inv_freq[None, :]
    cos, sin = jnp.cos(ang), jnp.sin(ang)
    return jnp.concatenate([cos, cos], axis=1), jnp.concatenate([-sin, sin], axis=1)


def _in_proj(h, w_in, rope, d, *, tm, tn, precise=False):
    kvw = d // 4
    mm = functools.partial(_matmul, h, w_in, tm=tm, tn=tn, precise=precise)
    qa = mm(0, d, rope=rope)
    ka = mm(d, kvw, rope=rope)
    va = mm(d + kvw, kvw)
    qb = mm(d + 2 * kvw, d, rope=rope)
    kb = mm(2 * d + 2 * kvw, kvw, rope=rope)
    vb = mm(2 * d + 3 * kvw, kvw)
    gates = mm(2 * d + 4 * kvw, 2 * d)
    return qa, ka, va, qb, kb, vb, gates


def kernel(x_prompt, x_sample, c_prompt, c_sample, cache_moba_k, cache_moba_v, cache_diff_k, cache_diff_v, page_table, w_ada, b_ada, g_pre_mix, g_post_mix, g_pre_ffn, g_post_ffn, w_in, w_out, lambda_q1, lambda_k1, lambda_q2, lambda_k2, diff_subln, w_router, b_router, w_mlp1, b_mlp1, w_mlp2, b_mlp2):
    depth = w_in.shape[0]
    assert depth == 1 and x_prompt.shape[0] == 1 and x_sample.shape[1] == 1
    _, seq, d = x_prompt.shape
    nseq = x_sample.shape[0]
    n_pool = cache_moba_k.shape[1]
    past = page_table.shape[1] * PAGE_SIZE
    lam_init = 0.8 - 0.6 * math.exp(-0.3 * 0)
    row = lambda v: v[0][None, :]
    lam_params = [row(lambda_q1), row(lambda_k1), row(lambda_q2), row(lambda_k2)]
    subln = row(diff_subln)
    nh_a = d // HEAD_DIM
    kv_a = nh_a // MOBA_GROUP
    nh_b = d // (2 * HEAD_DIM)
    kv_b = nh_b // DIFF_GROUP

    c_all = jnp.concatenate([c_prompt, c_sample], axis=0)
    mod = _matmul(c_all, w_ada[0], 0, 6 * d, tm=c_all.shape[0], tn=512, bias=b_ada, silu_a=True, precise=True)
    mods_p = [mod[0:1, i * d:(i + 1) * d] for i in range(6)]
    mods_s = [mod[1:, i * d:(i + 1) * d] for i in range(6)]

    xp = x_prompt[0]
    shift_m, scale_m, gate_m, shift_f, scale_f, gate_f = mods_p
    h = _prenorm(xp, row(g_pre_mix), scale_m, shift_m, tm=256)
    rope_p = _rope_tables(jnp.arange(seq, dtype=jnp.int32))
    qa, ka, va, qb, kb, vb, gates = _in_proj(h, w_in[0], rope_p, d, tm=512, tn=512)
    o_a = _moba_prompt(qa, ka, va)
    o_b = _diff_prompt(qb, kb.astype(BF16), vb.astype(BF16), lam_params, subln, lam_init)
    z = _matmul(_merge(gates, o_a, o_b, tm=256), w_out[0], 0, d, tm=512, tn=512)
    x1_p, h2_p, lg_p = _post_mix(xp, z, gate_m, row(g_post_mix), row(g_pre_ffn), scale_f, shift_f,
                                 w_router[0], b_router, tm=256)
    gate_f_p = gate_f

    xs = x_sample[:, 0, :]
    shift_m, scale_m, gate_m, shift_f, scale_f, gate_f = mods_s
    h = _prenorm(xs, row(g_pre_mix), scale_m, shift_m, tm=nseq, dtype=F32)
    rope_s = _rope_tables(jnp.full((nseq,), past, jnp.int32))
    qa_s, ka_s, va_s, qb_s, kb_s, vb_s, gates_s = _in_proj(h, w_in[0], rope_s, d, tm=nseq, tn=512, precise=True)

    qf_a = qa_s.reshape(nseq, nh_a, HEAD_DIM)
    own_a = (jnp.arange(nh_a) // MOBA_GROUP)[:, None] == jnp.arange(kv_a)[None, :]
    qbd_a = jnp.where(own_a[None, :, :, None], qf_a[:, :, None, :], 0.0).reshape(nseq, nh_a, kv_a * HEAD_DIM)
    per_head = lambda t, width: jnp.repeat(t.reshape(nseq, -1, width), MOBA_GROUP, axis=1)
    o_a_s = _moba_decode(page_table, qbd_a, qf_a, per_head(ka_s, HEAD_DIM), per_head(va_s, HEAD_DIM),
                         cache_moba_k[0].reshape(n_pool, PAGE_SIZE * kv_a, HEAD_DIM),
                         cache_moba_v[0].reshape(n_pool, PAGE_SIZE * kv_a, HEAD_DIM)).reshape(nseq, d)

    qf_b = jnp.swapaxes(qb_s.reshape(nseq, nh_b, 2, HEAD_DIM), 1, 2).reshape(nseq, 2 * nh_b, HEAD_DIM)
    rr = jnp.arange(2 * nh_b)
    kcol = ((rr % nh_b) // DIFF_GROUP) * 2 + rr // nh_b
    own_b = kcol[:, None] == jnp.arange(2 * kv_b)[None, :]
    qbd_b = jnp.where(own_b[None, :, :, None], qf_b[:, :, None, :], 0.0).reshape(nseq, 2 * nh_b, 2 * kv_b * HEAD_DIM)
    knew_b = kb_s.reshape(nseq, 2 * kv_b, HEAD_DIM)[:, kcol, :]
    vnew_b = vb_s.reshape(nseq, kv_b, 2 * HEAD_DIM)[:, (rr % nh_b) // DIFF_GROUP, :]
    o_b_s = _diff_decode(page_table, qbd_b, qf_b, knew_b, vnew_b,
                         cache_diff_k[0].reshape(n_pool, PAGE_SIZE * 2 * kv_b, HEAD_DIM), cache_diff_v[0],
                         lam_params, subln, lam_init).reshape(nseq, d)

    z = _matmul(_merge(gates_s, o_a_s, o_b_s, tm=nseq, dtype=F32), w_out[0], 0, d, tm=nseq, tn=512, precise=True)
    x1_s, h2_s, lg_s = _post_mix(xs, z, gate_m, row(g_post_mix), row(g_pre_ffn), scale_f, shift_f,
                                 w_router[0], b_router, tm=nseq)

    moe = _moe(jnp.concatenate([h2_p, h2_s], axis=0), jnp.concatenate([lg_p, lg_s], axis=0),
               w_mlp1[0], b_mlp1[0], w_mlp2[0], b_mlp2[0])
    y_p = _post_ffn(x1_p, moe[:seq], gate_f_p, row(g_post_ffn), tm=256)
    y_s = _post_ffn(x1_s, moe[seq:seq + nseq], gate_f, row(g_post_ffn), tm=nseq)

    return (y_p[None], y_s[:, None, :],
            ka.reshape(1, 1, seq, kv_a, HEAD_DIM), va.reshape(1, 1, seq, kv_a, HEAD_DIM),
            kb.reshape(1, 1, seq, kv_b, 2, HEAD_DIM), vb.reshape(1, 1, seq, kv_b, 2 * HEAD_DIM),
            ka_s.reshape(1, nseq, 1, kv_a, HEAD_DIM), va_s.reshape(1, nseq, 1, kv_a, HEAD_DIM),
            kb_s.reshape(1, nseq, 1, kv_b, 2, HEAD_DIM), vb_s.reshape(1, nseq, 1, kv_b, 2 * HEAD_DIM))
```

```python
import functools
import math

import jax
import jax.numpy as jnp
from jax import lax
from jax.experimental import pallas as pl
from jax.experimental.pallas import tpu as pltpu

F32 = jnp.float32
BF16 = jnp.bfloat16

HEAD_DIM = 128
MOBA_BLOCK = 256
MOBA_TOPK = 3
MOBA_GROUP = 4
DIFF_GROUP = 4
PAGE_SIZE = 128
ROPE_THETA = 10000.0
TOP_K = 4
SWIGLU_ALPHA = 1.702
SWIGLU_LIMIT = 7.0
NORM_EPS = 1e-6
SUBLN_EPS = 1e-5
NEG_INF = -1e30
LANES = 128
VMEM_LIMIT = 56 * 1024 * 1024

EXPERT_ROWS = 1152
EXPERT_SUB = 384
EXPERT_TN = 256
ROUTE_TM = 256
COMBINE_TM = 64
DMA_WINDOW = 32


def _cparams(sem, vmem=VMEM_LIMIT):
    return pltpu.CompilerParams(dimension_semantics=sem, vmem_limit_bytes=vmem)


def _rms(x, eps):
    return x * lax.rsqrt(jnp.mean(x * x, axis=-1, keepdims=True) + eps)


def _mm_kernel(a_ref, w_ref, *rest, rope, bias, silu_a, precise):
    o_ref = rest[-1] if precise else rest[-2]
    a = a_ref[...]
    if silu_a:
        a = a * jax.nn.sigmoid(a)
    if precise:
        acc = jnp.dot(a, w_ref[...], precision=lax.Precision.HIGHEST, preferred_element_type=F32)
    else:
        wb_ref = rest[-1]

        @pl.when(pl.program_id(1) == 0)
        def _():
            wb_ref[...] = w_ref[...].astype(BF16)

        acc = jnp.dot(a.astype(BF16), wb_ref[...], preferred_element_type=F32)
    if bias:
        acc = acc + rest[0][...]
    if rope:
        cos, sin = rest[0][...], rest[1][...]
        for c in range(acc.shape[1] // HEAD_DIM):
            blk = acc[:, c * HEAD_DIM:(c + 1) * HEAD_DIM]
            o_ref[:, c * HEAD_DIM:(c + 1) * HEAD_DIM] = blk * cos + pltpu.roll(blk, HEAD_DIM // 2, 1) * sin
    else:
        o_ref[...] = acc


def _matmul(a, w, col0, ncols, *, tm, tn, rope=None, bias=None, silu_a=False, precise=False):
    m, k = a.shape
    assert precise == (a.dtype == F32)
    assert m % tm == 0 and ncols % tn == 0 and col0 % tn == 0
    assert rope is None or bias is None
    cb = col0 // tn
    in_specs = [pl.BlockSpec((tm, k), lambda j, i: (i, 0)),
                pl.BlockSpec((k, tn), lambda j, i: (0, cb + j))]
    args = [a, w]
    if rope is not None:
        in_specs += [pl.BlockSpec((tm, HEAD_DIM), lambda j, i: (i, 0))] * 2
        args += list(rope)
    if bias is not None:
        in_specs.append(pl.BlockSpec((1, tn), lambda j, i: (0, cb + j)))
        args.append(bias)
    return pl.pallas_call(
        functools.partial(_mm_kernel, rope=rope is not None, bias=bias is not None, silu_a=silu_a,
                          precise=precise),
        grid=(ncols // tn, m // tm),
        in_specs=in_specs,
        out_specs=pl.BlockSpec((tm, tn), lambda j, i: (i, j)),
        out_shape=jax.ShapeDtypeStruct((m, ncols), F32),
        scratch_shapes=[] if precise else [pltpu.VMEM((k, tn), BF16)],
        compiler_params=_cparams(("arbitrary", "arbitrary")),
    )(*args)


def _prenorm_kernel(x_ref, g_ref, scale_ref, shift_ref, h_ref):
    y = _rms(x_ref[...], NORM_EPS) * g_ref[...]
    h_ref[...] = (y * (1.0 + scale_ref[...]) + shift_ref[...]).astype(h_ref.dtype)


def _row_spec(per_row, tm, d):
    return pl.BlockSpec((tm, d), lambda i: (i, 0)) if per_row else pl.BlockSpec((1, d), lambda i: (0, 0))


def _prenorm(x, g, scale, shift, *, tm, dtype=BF16):
    m, d = x.shape
    per_row = scale.shape[0] != 1
    return pl.pallas_call(
        _prenorm_kernel,
        grid=(m // tm,),
        in_specs=[pl.BlockSpec((tm, d), lambda i: (i, 0)), pl.BlockSpec((1, d), lambda i: (0, 0)),
                  _row_spec(per_row, tm, d), _row_spec(per_row, tm, d)],
        out_specs=pl.BlockSpec((tm, d), lambda i: (i, 0)),
        out_shape=jax.ShapeDtypeStruct((m, d), dtype),
        compiler_params=_cparams(("arbitrary",)),
    )(x, g, scale, shift)


def _merge_kernel(ga_ref, gb_ref, oa_ref, ob_ref, o_ref):
    o_ref[...] = (jax.nn.sigmoid(ga_ref[...]) * oa_ref[...]
                  + jax.nn.sigmoid(gb_ref[...]) * ob_ref[...]).astype(o_ref.dtype)


def _merge(gates, o_a, o_b, *, tm, dtype=BF16):
    m, d = o_a.shape
    tc = min(d, 1024)
    nc = d // tc
    return pl.pallas_call(
        _merge_kernel,
        grid=(m // tm, nc),
        in_specs=[pl.BlockSpec((tm, tc), lambda i, j: (i, j)),
                  pl.BlockSpec((tm, tc), lambda i, j: (i, nc + j)),
                  pl.BlockSpec((tm, tc), lambda i, j: (i, j)),
                  pl.BlockSpec((tm, tc), lambda i, j: (i, j))],
        out_specs=pl.BlockSpec((tm, tc), lambda i, j: (i, j)),
        out_shape=jax.ShapeDtypeStruct((m, d), dtype),
        compiler_params=_cparams(("arbitrary", "arbitrary")),
    )(gates, gates, o_a, o_b)


def _post_mix_kernel(x_ref, z_ref, gate_ref, gpost_ref, gpre_ref, scale_ref, shift_ref, wr_ref, br_ref,
                     x1_ref, h2_ref, lg_ref):
    x1 = x_ref[...] + gate_ref[...] * (_rms(z_ref[...], NORM_EPS) * gpost_ref[...])
    x1_ref[...] = x1
    h2 = _rms(x1, NORM_EPS) * gpre_ref[...] * (1.0 + scale_ref[...]) + shift_ref[...]
    h2_ref[...] = h2.astype(h2_ref.dtype)
    lg_ref[...] = jnp.dot(h2, wr_ref[...], precision=lax.Precision.HIGHEST,
                          preferred_element_type=F32) + br_ref[...]


def _post_mix(x, z, gate, g_post, g_pre, scale, shift, w_router, b_router, *, tm):
    m, d = x.shape
    per_row = gate.shape[0] != 1
    ne = w_router.shape[1]
    row = pl.BlockSpec((tm, d), lambda i: (i, 0))
    vec = pl.BlockSpec((1, d), lambda i: (0, 0))
    mod = _row_spec(per_row, tm, d)
    return pl.pallas_call(
        _post_mix_kernel,
        grid=(m // tm,),
        in_specs=[row, row, mod, vec, vec, mod, mod,
                  pl.BlockSpec((d, ne), lambda i: (0, 0)), pl.BlockSpec((1, ne), lambda i: (0, 0))],
        out_specs=[row, row, pl.BlockSpec((tm, ne), lambda i: (i, 0))],
        out_shape=[jax.ShapeDtypeStruct((m, d), F32), jax.ShapeDtypeStruct((m, d), BF16),
                   jax.ShapeDtypeStruct((m, ne), F32)],
        compiler_params=_cparams(("arbitrary",)),
    )(x, z, gate, g_post, g_pre, scale, shift, w_router, b_router)


def _post_ffn_kernel(x_ref, z_ref, gate_ref, gpost_ref, y_ref):
    y_ref[...] = x_ref[...] + gate_ref[...] * (_rms(z_ref[...], NORM_EPS) * gpost_ref[...])


def _post_ffn(x, z, gate, g_post, *, tm):
    m, d = x.shape
    per_row = gate.shape[0] != 1
    row = pl.BlockSpec((tm, d), lambda i: (i, 0))
    return pl.pallas_call(
        _post_ffn_kernel,
        grid=(m // tm,),
        in_specs=[row, row, _row_spec(per_row, tm, d), pl.BlockSpec((1, d), lambda i: (0, 0))],
        out_specs=row,
        out_shape=jax.ShapeDtypeStruct((m, d), F32),
        compiler_params=_cparams(("arbitrary",)),
    )(x, z, gate, g_post)


def _topk_mask(score, k, n_valid_lanes):
    lane = lax.broadcasted_iota(jnp.int32, score.shape, 1)
    sel = jnp.zeros(score.shape, F32)
    vals, idxs = [], []
    for _ in range(k):
        m = jnp.max(score, axis=1, keepdims=True)
        idx = jnp.min(jnp.where(score == m, lane, n_valid_lanes), axis=1, keepdims=True)
        pick = lane == idx
        sel = jnp.where(pick & (m > 0.5 * NEG_INF), 1.0, sel)
        score = jnp.where(pick, -jnp.inf, score)
        vals.append(m)
        idxs.append(idx)
    return sel, vals, idxs


def _stack_heads(x, n, width, offset=0, stride=None):
    stride = width if stride is None else stride
    return jnp.concatenate([x[:, h * stride + offset:h * stride + offset + width] for h in range(n)], axis=0)


def _softmax_step(s, v, m_ref, l_ref, acc_ref):
    m_prev = m_ref[...]
    m_new = jnp.maximum(m_prev, jnp.max(s, axis=1, keepdims=True))
    alpha = jnp.exp(m_prev - m_new)
    p = jnp.exp(s - m_new)
    l_ref[...] = alpha * l_ref[...] + jnp.sum(p, axis=1, keepdims=True)
    acc_ref[...] = alpha * acc_ref[...] + jnp.dot(p.astype(BF16), v, preferred_element_type=F32)
    m_ref[...] = m_new


def _causal_mask(rows, blk):
    r = lax.broadcasted_iota(jnp.int32, (rows, blk), 0) & (blk - 1)
    c = lax.broadcasted_iota(jnp.int32, (rows, blk), 1)
    return c <= r


def _qk(q, k):
    return lax.dot_general(q, k, (((1,), (1,)), ((), ())), preferred_element_type=F32)


def _moba_prompt_kernel(q_ref, k_ref, v_ref, o_ref, kb_ref, vb_ref, kmean_ref, m_ref, l_ref, acc_ref):
    n = pl.program_id(1)
    blk = MOBA_BLOCK
    nb = k_ref.shape[0] // blk
    scale = HEAD_DIM ** -0.5

    @pl.when(n == 0)
    def _():
        kb_ref[...] = k_ref[...].astype(BF16)
        vb_ref[...] = v_ref[...].astype(BF16)
        for b in range(nb):
            kmean_ref[b:b + 1, :] = jnp.mean(k_ref[b * blk:(b + 1) * blk, :], axis=0, keepdims=True)

    q4 = _stack_heads(q_ref[...], MOBA_GROUP, HEAD_DIM)
    gate = lax.dot_general(q4, kmean_ref[...], (((1,), (1,)), ((), ())),
                           precision=lax.Precision.HIGHEST, preferred_element_type=F32)
    lane = lax.broadcasted_iota(jnp.int32, gate.shape, 1)
    gate = jnp.where(lane < n, gate, NEG_INF)
    sel, _, _ = _topk_mask(gate, min(MOBA_TOPK, nb - 1), nb)

    qb = (q4 * scale).astype(BF16)
    own = pl.multiple_of(n * blk, blk)
    s = _qk(qb, kb_ref[pl.ds(own, blk), :])
    s = jnp.where(_causal_mask(MOBA_GROUP * blk, blk), s, NEG_INF)
    m0 = jnp.max(s, axis=1, keepdims=True)
    p = jnp.exp(s - m0)
    m_ref[...] = m0
    l_ref[...] = jnp.sum(p, axis=1, keepdims=True)
    acc_ref[...] = jnp.dot(p.astype(BF16), vb_ref[pl.ds(own, blk), :], preferred_element_type=F32)

    def body(j, carry):
        picked = jnp.sum(jnp.where(lane == j, sel, 0.0), axis=1, keepdims=True)
        off = pl.multiple_of(j * blk, blk)
        sj = _qk(qb, kb_ref[pl.ds(off, blk), :])
        sj = jnp.where(picked > 0.0, sj, NEG_INF)
        _softmax_step(sj, vb_ref[pl.ds(off, blk), :], m_ref, l_ref, acc_ref)
        return carry

    lax.fori_loop(0, n, body, 0)
    out = acc_ref[...] / l_ref[...]
    for h in range(MOBA_GROUP):
        o_ref[:, h * HEAD_DIM:(h + 1) * HEAD_DIM] = out[h * blk:(h + 1) * blk, :]


def _moba_prompt(q, k, v):
    s, dq = q.shape
    kv = k.shape[1] // HEAD_DIM
    blk = MOBA_BLOCK
    assert s % blk == 0 and dq == kv * MOBA_GROUP * HEAD_DIM
    rows = MOBA_GROUP * blk
    return pl.pallas_call(
        _moba_prompt_kernel,
        grid=(kv, s // blk),
        in_specs=[pl.BlockSpec((blk, MOBA_GROUP * HEAD_DIM), lambda g, n: (n, g)),
                  pl.BlockSpec((s, HEAD_DIM), lambda g, n: (0, g)),
                  pl.BlockSpec((s, HEAD_DIM), lambda g, n: (0, g))],
        out_specs=pl.BlockSpec((blk, MOBA_GROUP * HEAD_DIM), lambda g, n: (n, g)),
        out_shape=jax.ShapeDtypeStruct((s, dq), F32),
        scratch_shapes=[pltpu.VMEM((s, HEAD_DIM), BF16), pltpu.VMEM((s, HEAD_DIM), BF16),
                        pltpu.VMEM((s // blk, HEAD_DIM), F32),
                        pltpu.VMEM((rows, 1), F32), pltpu.VMEM((rows, 1), F32),
                        pltpu.VMEM((rows, HEAD_DIM), F32)],
        compiler_params=_cparams(("arbitrary", "arbitrary")),
    )(q, k, v)


def _lambda_full(lq1_ref, lk1_ref, lq2_ref, lk2_ref, lam_init):
    return (jnp.exp(jnp.sum(lq1_ref[...] * lk1_ref[...], axis=1, keepdims=True))
            - jnp.exp(jnp.sum(lq2_ref[...] * lk2_ref[...], axis=1, keepdims=True)) + lam_init)


def _diff_prompt_kernel(q_ref, k_ref, v_ref, lq1_ref, lk1_ref, lq2_ref, lk2_ref, sub_ref, o_ref,
                        m1_ref, l1_ref, acc1_ref, m2_ref, l2_ref, acc2_ref, *, lam_init, tq):
    n = pl.program_id(1)
    scale = HEAD_DIM ** -0.5
    dv = 2 * HEAD_DIM
    q = q_ref[...] * scale
    q1 = _stack_heads(q, DIFF_GROUP, HEAD_DIM, 0, dv).astype(BF16)
    q2 = _stack_heads(q, DIFF_GROUP, HEAD_DIM, HEAD_DIM, dv).astype(BF16)
    for m_ref, l_ref, acc_ref in ((m1_ref, l1_ref, acc1_ref), (m2_ref, l2_ref, acc2_ref)):
        m_ref[...] = jnp.full(m_ref.shape, NEG_INF, F32)
        l_ref[...] = jnp.zeros(l_ref.shape, F32)
        acc_ref[...] = jnp.zeros(acc_ref.shape, F32)

    def step(off, mask):
        kj = k_ref[pl.ds(off, tq), :]
        vj = v_ref[pl.ds(off, tq), :]
        s1 = _qk(q1, kj[:, :HEAD_DIM])
        s2 = _qk(q2, kj[:, HEAD_DIM:])
        if mask is not None:
            s1 = jnp.where(mask, s1, NEG_INF)
            s2 = jnp.where(mask, s2, NEG_INF)
        _softmax_step(s1, vj, m1_ref, l1_ref, acc1_ref)
        _softmax_step(s2, vj, m2_ref, l2_ref, acc2_ref)

    def body(j, carry):
        step(pl.multiple_of(j * tq, tq), None)
        return carry

    lax.fori_loop(0, n, body, 0)
    step(pl.multiple_of(n * tq, tq), _causal_mask(DIFF_GROUP * tq, tq))

    lam = _lambda_full(lq1_ref, lk1_ref, lq2_ref, lk2_ref, lam_init)
    o = acc1_ref[...] / l1_ref[...] - lam * (acc2_ref[...] / l2_ref[...])
    o = _rms(o, SUBLN_EPS) * sub_ref[...] * (1.0 - lam_init)
    for h in range(DIFF_GROUP):
        o_ref[:, h * dv:(h + 1) * dv] = o[h * tq:(h + 1) * tq, :]


def _diff_prompt(q, k, v, lam_params, subln, lam_init, *, tq=256):
    s, dq = q.shape
    dv = 2 * HEAD_DIM
    kv = k.shape[1] // dv
    assert s % tq == 0 and dq == kv * DIFF_GROUP * dv
    rows = DIFF_GROUP * tq
    vec = pl.BlockSpec((1, HEAD_DIM), lambda g, n: (0, 0))
    stat = pltpu.VMEM((rows, 1), F32)
    acc = pltpu.VMEM((rows, dv), F32)
    return pl.pallas_call(
        functools.partial(_diff_prompt_kernel, lam_init=lam_init, tq=tq),
        grid=(kv, s // tq),
        in_specs=[pl.BlockSpec((tq, DIFF_GROUP * dv), lambda g, n: (n, g)),
                  pl.BlockSpec((s, dv), lambda g, n: (0, g)),
                  pl.BlockSpec((s, dv), lambda g, n: (0, g)),
                  vec, vec, vec, vec, pl.BlockSpec((1, dv), lambda g, n: (0, 0))],
        out_specs=pl.BlockSpec((tq, DIFF_GROUP * dv), lambda g, n: (n, g)),
        out_shape=jax.ShapeDtypeStruct((s, dq), F32),
        scratch_shapes=[stat, stat, acc, stat, stat, acc],
        compiler_params=_cparams(("arbitrary", "arbitrary")),
    )(q, k, v, *lam_params, subln)


def _load_heads(ref, n_heads, rows):
    return jnp.concatenate([ref[0, pl.ds(h, rows, stride=n_heads), :] for h in range(n_heads)], axis=1)


def _diag_blocks(x, width, group, n_groups):
    out = jnp.zeros((x.shape[0], width), x.dtype)
    for g in range(n_groups):
        out = jnp.where(group == g, x[:, g * width:(g + 1) * width], out)
    return out


def _split_bf16(x):
    hi = x.astype(BF16)
    return hi, (x - hi.astype(F32)).astype(BF16)


def _dot3(a, b, mm):
    return mm(a[0], b[0]) + (mm(a[1], b[0]) + mm(a[0], b[1]))


def _page_scores(q, k0, k1):
    qs = _split_bf16(q)
    return jnp.concatenate([_dot3(qs, _split_bf16(k0), _qk), _dot3(qs, _split_bf16(k1), _qk)], axis=1)


def _page_pv(p, v0, v1):
    half = p.shape[1] // 2
    mm = functools.partial(jnp.dot, preferred_element_type=F32)
    return (_dot3(_split_bf16(p[:, :half]), _split_bf16(v0), mm)
            + _dot3(_split_bf16(p[:, half:]), _split_bf16(v1), mm))


def _moba_decode_kernel(pt_ref, qbd_ref, qf_ref, knew_ref, vnew_ref, k0_ref, k1_ref, v0_ref, v1_ref, o_ref,
                        m_ref, l_ref, g_ref, oall_ref, *, nblk, kv):
    del pt_ref
    j = pl.program_id(1)
    scale = HEAD_DIM ** -0.5

    @pl.when(j == 0)
    def _():
        m_ref[...] = jnp.full(m_ref.shape, NEG_INF, F32)
        l_ref[...] = jnp.zeros(l_ref.shape, F32)
        g_ref[...] = jnp.full(g_ref.shape, NEG_INF, F32)

    qbd = qbd_ref[0]
    k0 = _load_heads(k0_ref, kv, PAGE_SIZE)
    k1 = _load_heads(k1_ref, kv, PAGE_SIZE)
    s = _page_scores(qbd * scale, k0, k1)
    ksum = jnp.sum(k0, axis=0, keepdims=True) + jnp.sum(k1, axis=0, keepdims=True)
    gate = jnp.sum(qbd * ksum, axis=1, keepdims=True) * (1.0 / MOBA_BLOCK)
    m = jnp.max(s, axis=1, keepdims=True)
    p = jnp.exp(s - m)
    pv = _page_pv(p, _load_heads(v0_ref, kv, PAGE_SIZE), _load_heads(v1_ref, kv, PAGE_SIZE))
    nh = qbd.shape[0]
    hgrp = lax.broadcasted_iota(jnp.int32, (nh, HEAD_DIM), 0) // MOBA_GROUP
    lane = lax.broadcasted_iota(jnp.int32, (nh, LANES), 1)
    m_ref[...] = jnp.where(lane == j, m, m_ref[...])
    l_ref[...] = jnp.where(lane == j, jnp.sum(p, axis=1, keepdims=True), l_ref[...])
    g_ref[...] = jnp.where(lane == j, gate, g_ref[...])
    oall_ref[j] = _diag_blocks(pv, HEAD_DIM, hgrp, kv)

    @pl.when(j == nblk - 1)
    def _():
        sel, _, _ = _topk_mask(g_ref[...], min(MOBA_TOPK, nblk), LANES)
        s_new = jnp.sum(qf_ref[0] * knew_ref[0], axis=1, keepdims=True) * scale
        m_all = m_ref[...]
        mx = jnp.maximum(jnp.max(jnp.where(sel > 0.0, m_all, NEG_INF), axis=1, keepdims=True), s_new)
        w = jnp.where(sel > 0.0, jnp.exp(m_all - mx), 0.0)
        e_new = jnp.exp(s_new - mx)
        den = jnp.sum(w * l_ref[...], axis=1, keepdims=True) + e_new
        num = e_new * vnew_ref[0]
        for b in range(nblk):
            num = num + w[:, b:b + 1] * oall_ref[b]
        o_ref[0] = num / den


def _paged_specs(npages, rows_per_page):
    def spec(which):
        return pl.BlockSpec((1, rows_per_page, HEAD_DIM), lambda b, j, pt: (pt[b * npages + 2 * j + which], 0, 0))
    return spec(0), spec(1)


def _moba_decode(page_table, qbd, qf, knew, vnew, cache_k, cache_v):
    nseq, nh, _ = qf.shape
    npages = page_table.shape[1]
    kv = cache_k.shape[1] // PAGE_SIZE
    nblk = npages * PAGE_SIZE // MOBA_BLOCK
    assert MOBA_BLOCK == 2 * PAGE_SIZE and nblk <= LANES
    p0, p1 = _paged_specs(npages, PAGE_SIZE * kv)
    per_seq = lambda w: pl.BlockSpec((1, nh, w), lambda b, j, pt: (b, 0, 0))
    gs = pltpu.PrefetchScalarGridSpec(
        num_scalar_prefetch=1, grid=(nseq, nblk),
        in_specs=[per_seq(kv * HEAD_DIM), per_seq(HEAD_DIM), per_seq(HEAD_DIM), per_seq(HEAD_DIM),
                  p0, p1, p0, p1],
        out_specs=per_seq(HEAD_DIM),
        scratch_shapes=[pltpu.VMEM((nh, LANES), F32), pltpu.VMEM((nh, LANES), F32), pltpu.VMEM((nh, LANES), F32),
                        pltpu.VMEM((nblk, nh, HEAD_DIM), F32)])
    return pl.pallas_call(
        functools.partial(_moba_decode_kernel, nblk=nblk, kv=kv),
        grid_spec=gs,
        out_shape=jax.ShapeDtypeStruct((nseq, nh, HEAD_DIM), F32),
        compiler_params=_cparams(("arbitrary", "arbitrary")),
    )(page_table.reshape(-1), qbd, qf, knew, vnew, cache_k, cache_k, cache_v, cache_v)


def _diff_decode_kernel(pt_ref, qbd_ref, qf_ref, knew_ref, vnew_ref, k0_ref, k1_ref, v0_ref, v1_ref,
                        lq1_ref, lk1_ref, lq2_ref, lk2_ref, sub_ref, o_ref, m_ref, l_ref, acc_ref,
                        *, nsteps, kv, lam_init):
    del pt_ref
    j = pl.program_id(1)
    scale = HEAD_DIM ** -0.5
    dv = 2 * HEAD_DIM

    @pl.when(j == 0)
    def _():
        m_ref[...] = jnp.full(m_ref.shape, NEG_INF, F32)
        l_ref[...] = jnp.zeros(l_ref.shape, F32)
        acc_ref[...] = jnp.zeros(acc_ref.shape, F32)

    q = qbd_ref[0] * scale
    s = _page_scores(q, _load_heads(k0_ref, 2 * kv, PAGE_SIZE), _load_heads(k1_ref, 2 * kv, PAGE_SIZE))
    v0 = jnp.concatenate([v0_ref[0, :, g, :] for g in range(kv)], axis=1)
    v1 = jnp.concatenate([v1_ref[0, :, g, :] for g in range(kv)], axis=1)
    m_prev = m_ref[...]
    m_new = jnp.maximum(m_prev, jnp.max(s, axis=1, keepdims=True))
    alpha = jnp.exp(m_prev - m_new)
    p = jnp.exp(s - m_new)
    l_ref[...] = alpha * l_ref[...] + jnp.sum(p, axis=1, keepdims=True)
    acc_ref[...] = alpha * acc_ref[...] + _page_pv(p, v0, v1)
    m_ref[...] = m_new

    @pl.when(j == nsteps - 1)
    def _():
        rows = acc_ref.shape[0]
        nh = rows // 2
        hgrp = (lax.broadcasted_iota(jnp.int32, (rows, dv), 0) & (nh - 1)) // DIFF_GROUP
        acc = _diag_blocks(acc_ref[...], dv, hgrp, kv)
        s_new = jnp.sum(qf_ref[0] * knew_ref[0], axis=1, keepdims=True) * scale
        mx = jnp.maximum(m_ref[...], s_new)
        a = jnp.exp(m_ref[...] - mx)
        e = jnp.exp(s_new - mx)
        o2 = (a * acc + e * vnew_ref[0]) / (a * l_ref[...] + e)
        lam = _lambda_full(lq1_ref, lk1_ref, lq2_ref, lk2_ref, lam_init)
        o = o2[:nh] - lam * o2[nh:]
        o_ref[0] = _rms(o, SUBLN_EPS) * sub_ref[...] * (1.0 - lam_init)


def _diff_decode(page_table, qbd, qf, knew, vnew, cache_k, cache_v, lam_params, subln, lam_init):
    nseq, rows, _ = qf.shape
    npages = page_table.shape[1]
    kv = cache_v.shape[2]
    dv = 2 * HEAD_DIM
    assert npages % 2 == 0 and rows & (rows - 1) == 0
    nsteps = npages // 2
    k0, k1 = _paged_specs(npages, PAGE_SIZE * 2 * kv)
    vspec = lambda which: pl.BlockSpec((1, PAGE_SIZE, kv, dv),
                                       lambda b, j, pt: (pt[b * npages + 2 * j + which], 0, 0, 0))
    per_seq = lambda r, w: pl.BlockSpec((1, r, w), lambda b, j, pt: (b, 0, 0))
    vec = pl.BlockSpec((1, HEAD_DIM), lambda b, j, pt: (0, 0))
    gs = pltpu.PrefetchScalarGridSpec(
        num_scalar_prefetch=1, grid=(nseq, nsteps),
        in_specs=[per_seq(rows, 2 * kv * HEAD_DIM), per_seq(rows, HEAD_DIM), per_seq(rows, HEAD_DIM),
                  per_seq(rows, dv), k0, k1, vspec(0), vspec(1), vec, vec, vec, vec,
                  pl.BlockSpec((1, dv), lambda b, j, pt: (0, 0))],
        out_specs=per_seq(rows // 2, dv),
        scratch_shapes=[pltpu.VMEM((rows, 1), F32), pltpu.VMEM((rows, 1), F32),
                        pltpu.VMEM((rows, kv * dv), F32)])
    return pl.pallas_call(
        functools.partial(_diff_decode_kernel, nsteps=nsteps, kv=kv, lam_init=lam_init),
        grid_spec=gs,
        out_shape=jax.ShapeDtypeStruct((nseq, rows // 2, dv), F32),
        compiler_params=_cparams(("arbitrary", "arbitrary")),
    )(page_table.reshape(-1), qbd, qf, knew, vnew, cache_k, cache_k, cache_v, cache_v, *lam_params, subln)


def _route_kernel(lg_ref, idx_ref, wts_ref, rank_ref, cnt_ref, *, n_tokens, tm):
    i = pl.program_id(0)

    @pl.when(i == 0)
    def _():
        cnt_ref[...] = jnp.zeros(cnt_ref.shape, F32)

    lg = lg_ref[...]
    ne = lg.shape[1]
    sel, vals, idxs = _topk_mask(lg, TOP_K, ne)
    row = i * tm + lax.broadcasted_iota(jnp.int32, (tm, 1), 0)
    sel = jnp.where(row < n_tokens, sel, 0.0)
    ex = [jnp.exp(v - vals[0]) for v in vals]
    den = functools.reduce(lambda a, b: a + b, ex)
    r = lax.broadcasted_iota(jnp.int32, (tm, tm), 0)
    c = lax.broadcasted_iota(jnp.int32, (tm, tm), 1)
    earlier = jnp.where(c < r, 1.0, 0.0).astype(BF16)
    before = jnp.dot(earlier, sel.astype(BF16), preferred_element_type=F32) + cnt_ref[...]
    lane = lax.broadcasted_iota(jnp.int32, (tm, ne), 1)
    lane_k = lax.broadcasted_iota(jnp.int32, (tm, TOP_K), 1)
    idx_o = jnp.zeros((tm, TOP_K), jnp.int32)
    rank_o = jnp.zeros((tm, TOP_K), jnp.int32)
    w_o = jnp.zeros((tm, TOP_K), F32)
    for k in range(TOP_K):
        rank = jnp.sum(jnp.where(lane == idxs[k], before, 0.0), axis=1, keepdims=True)
        idx_o = jnp.where(lane_k == k, idxs[k], idx_o)
        rank_o = jnp.where(lane_k == k, rank.astype(jnp.int32), rank_o)
        w_o = jnp.where(lane_k == k, ex[k] / den, w_o)
    idx_ref[...] = idx_o
    rank_ref[...] = rank_o
    wts_ref[...] = w_o
    cnt_ref[...] = cnt_ref[...] + jnp.sum(sel, axis=0, keepdims=True)


def _route(logits, n_tokens):
    npad, ne = logits.shape
    tm = ROUTE_TM
    out4 = pl.BlockSpec((tm, TOP_K), lambda i: (i, 0))
    return pl.pallas_call(
        functools.partial(_route_kernel, n_tokens=n_tokens, tm=tm),
        grid=(npad // tm,),
        in_specs=[pl.BlockSpec((tm, ne), lambda i: (i, 0))],
        out_specs=[out4, out4, out4, pl.BlockSpec((1, ne), lambda i: (0, 0))],
        out_shape=[jax.ShapeDtypeStruct((npad, TOP_K), jnp.int32), jax.ShapeDtypeStruct((npad, TOP_K), F32),
                   jax.ShapeDtypeStruct((npad, TOP_K), jnp.int32), jax.ShapeDtypeStruct((1, ne), F32)],
        compiler_params=_cparams(("arbitrary",)),
    )(logits)


def _scatter_kernel(pos_ref, h_ref, xin_ref, xout_ref, sem, *, n_tokens, tm):
    del xin_ref
    base = pl.program_id(0) * tm
    n_here = jnp.minimum(n_tokens - base, tm)

    def copy(r, k):
        return pltpu.make_async_copy(h_ref.at[r], xout_ref.at[pos_ref[TOP_K * (base + r) + k]], sem)

    def start(r, carry):
        for k in range(TOP_K):
            copy(r, k).start()
        return carry

    def wait(r, carry):
        for k in range(TOP_K):
            copy(r, k).wait()
        return carry

    lax.fori_loop(0, n_here, start, 0)
    lax.fori_loop(0, n_here, wait, 0)


def _scatter_rows(pos_flat, h3, n_rows, n_tokens):
    n_pad, s, l = h3.shape
    tm = COMBINE_TM
    zeros = jnp.zeros((n_rows, s, l), h3.dtype)
    anyspec = pl.BlockSpec(memory_space=pl.ANY)
    gs = pltpu.PrefetchScalarGridSpec(
        num_scalar_prefetch=1, grid=(-(-n_tokens // tm),),
        in_specs=[pl.BlockSpec((tm, s, l), lambda i, pos: (i, 0, 0)), anyspec], out_specs=anyspec,
        scratch_shapes=[pltpu.SemaphoreType.DMA(())])
    return pl.pallas_call(
        functools.partial(_scatter_kernel, n_tokens=n_tokens, tm=tm),
        grid_spec=gs,
        out_shape=jax.ShapeDtypeStruct((n_rows, s, l), h3.dtype),
        input_output_aliases={2: 0},
        compiler_params=_cparams(("arbitrary",)),
    )(pos_flat, h3, zeros)


def _swiglu_pairs(u):
    gl = jnp.minimum(u, SWIGLU_LIMIT)
    lin = jnp.clip(pltpu.roll(u, LANES - 1, 1), -SWIGLU_LIMIT, SWIGLU_LIMIT)
    even = (lax.broadcasted_iota(jnp.int32, u.shape, 1) & 1) == 0
    return jnp.where(even, gl * jax.nn.sigmoid(SWIGLU_ALPHA * gl) * (lin + 1.0), 0.0)


def _expert_kernel(gblk_ref, gexp_ref, grows_ref, x_ref, w1_ref, b1_ref, w2_ref, b2_ref, o_ref,
                   wb_ref, act_ref, *, n1):
    del gblk_ref, gexp_ref
    g = pl.program_id(0)
    s = pl.program_id(1)
    rows = grows_ref[g]
    tn = EXPERT_TN
    sub = EXPERT_SUB
    half = tn // 2
    d_in = x_ref.shape[1]
    d_ff = w2_ref.shape[1]

    @pl.when((rows > 0) & (s < n1))
    def _():
        wb_ref[0:d_in, :] = w1_ref[0].astype(BF16)
        rr = lax.broadcasted_iota(jnp.int32, (tn, half), 0)
        cc = lax.broadcasted_iota(jnp.int32, (tn, half), 1)
        pick_even = jnp.where(rr == 2 * cc, 1.0, 0.0).astype(BF16)
        for t in range(EXPERT_ROWS // sub):
            @pl.when(t * sub < rows)
            def _():
                u = jnp.dot(x_ref[t * sub:(t + 1) * sub, :], wb_ref[0:d_in, :],
                            preferred_element_type=F32) + b1_ref[0]
                a = jnp.concatenate([_swiglu_pairs(u[:, c * LANES:(c + 1) * LANES])
                                     for c in range(tn // LANES)], axis=1).astype(BF16)
                a = jnp.dot(a, pick_even, preferred_element_type=F32).astype(BF16)

                @pl.when(s % 2 == 0)
                def _():
                    act_ref[s // 2, t * sub:(t + 1) * sub, 0:half] = a

                @pl.when(s % 2 == 1)
                def _():
                    act_ref[s // 2, t * sub:(t + 1) * sub, half:tn] = a

    @pl.when((rows > 0) & (s >= n1))
    def _():
        wb_ref[0:d_ff, :] = w2_ref[0].astype(BF16)
        for t in range(EXPERT_ROWS // sub):
            @pl.when(t * sub < rows)
            def _():
                acc = jnp.zeros((sub, tn), F32) + b2_ref[0]
                for c in range(d_ff // tn):
                    acc = acc + jnp.dot(act_ref[c, t * sub:(t + 1) * sub, :], wb_ref[c * tn:(c + 1) * tn, :],
                                        preferred_element_type=F32)
                o_ref[t * sub:(t + 1) * sub, :] = acc


def _experts(gblk, gexp, grows, xs, w1, b1, w2, b2):
    n_rows, d = xs.shape
    ne, _, f2 = w1.shape
    d_ff = w2.shape[1]
    tn = EXPERT_TN
    n1, n2 = f2 // tn, d // tn
    last = n1 + n2 - 1
    assert f2 == 2 * d_ff and n1 % 2 == 0 and d_ff % tn == 0 and n_rows % EXPERT_ROWS == 0

    def step(g, s, gr):
        return jnp.where(gr[g] > 0, s, last)

    gs = pltpu.PrefetchScalarGridSpec(
        num_scalar_prefetch=3, grid=(n_rows // EXPERT_ROWS, n1 + n2),
        in_specs=[
            pl.BlockSpec((EXPERT_ROWS, d), lambda g, s, gb, ge, gr: (gb[g], 0)),
            pl.BlockSpec((1, d, tn), lambda g, s, gb, ge, gr: (ge[g], 0, jnp.minimum(step(g, s, gr), n1 - 1))),
            pl.BlockSpec((1, 1, tn), lambda g, s, gb, ge, gr: (ge[g], 0, jnp.minimum(step(g, s, gr), n1 - 1))),
            pl.BlockSpec((1, d_ff, tn), lambda g, s, gb, ge, gr: (ge[g], 0, jnp.maximum(step(g, s, gr) - n1, 0))),
            pl.BlockSpec((1, 1, tn), lambda g, s, gb, ge, gr: (ge[g], 0, jnp.maximum(step(g, s, gr) - n1, 0))),
        ],
        out_specs=pl.BlockSpec((EXPERT_ROWS, tn),
                               lambda g, s, gb, ge, gr: (gb[g], jnp.maximum(step(g, s, gr) - n1, 0))),
        scratch_shapes=[pltpu.VMEM((max(d, d_ff), tn), BF16),
                        pltpu.VMEM((n1 // 2, EXPERT_ROWS, tn), BF16)])
    return pl.pallas_call(
        functools.partial(_expert_kernel, n1=n1),
        grid_spec=gs,
        out_shape=jax.ShapeDtypeStruct((n_rows, d), F32),
        compiler_params=_cparams(("arbitrary", "arbitrary")),
    )(gblk, gexp, grows, xs, w1, b1.reshape(ne, 1, f2), w2, b2.reshape(ne, 1, d))


def _combine_kernel(pos_ref, w_ref, y_ref, o_ref, buf_ref, sem, *, tm):
    base = pl.program_id(0) * tm

    def copy(r, k):
        return pltpu.make_async_copy(y_ref.at[pos_ref[TOP_K * (base + r) + k]], buf_ref.at[k, r], sem)

    def start(r, carry):
        for k in range(TOP_K):
            copy(r, k).start()
        return carry

    def wait(r, carry):
        for k in range(TOP_K):
            copy(r, k).wait()
        return carry

    lax.fori_loop(0, tm, start, 0)
    lax.fori_loop(0, tm, wait, 0)

    def mix(r, carry):
        acc = w_ref[TOP_K * (base + r)] * buf_ref[0, r]
        for k in range(1, TOP_K):
            acc = acc + w_ref[TOP_K * (base + r) + k] * buf_ref[k, r]
        o_ref[r] = acc
        return carry

    lax.fori_loop(0, tm, mix, 0)


def _combine(pos_flat, w_flat, y3, n_pad):
    _, s, l = y3.shape
    tm = COMBINE_TM
    gs = pltpu.PrefetchScalarGridSpec(
        num_scalar_prefetch=1, grid=(n_pad // tm,),
        in_specs=[pl.BlockSpec(memory_space=pltpu.SMEM), pl.BlockSpec(memory_space=pl.ANY)],
        out_specs=pl.BlockSpec((tm, s, l), lambda i, pos: (i, 0, 0)),
        scratch_shapes=[pltpu.VMEM((TOP_K, tm, s, l), F32), pltpu.SemaphoreType.DMA(())])
    return pl.pallas_call(
        functools.partial(_combine_kernel, tm=tm),
        grid_spec=gs,
        out_shape=jax.ShapeDtypeStruct((n_pad, s, l), F32),
        compiler_params=_cparams(("arbitrary",)),
    )(pos_flat, w_flat, y3)


def _moe(h2, logits, w1, b1, w2, b2):
    n, d = h2.shape
    ne = logits.shape[1]
    n_pad = -(-n // ROUTE_TM) * ROUTE_TM
    assert ROUTE_TM % COMBINE_TM == 0 and d % LANES == 0
    idx, wts, rank, cnt = _route(jnp.pad(logits, ((0, n_pad - n), (0, 0))), n)

    cnt = cnt[0].astype(jnp.int32)
    nblk = (cnt + EXPERT_ROWS - 1) // EXPERT_ROWS
    blk_end = jnp.cumsum(nblk)
    blk_start = blk_end - nblk
    n_groups = (TOP_K * n) // EXPERT_ROWS + ne
    g = jnp.arange(n_groups, dtype=jnp.int32)
    used = g < blk_end[-1]
    gblk = jnp.where(used, g, blk_end[-1] - 1)
    gexp = jnp.minimum(jnp.searchsorted(blk_end, gblk, side="right"), ne - 1).astype(jnp.int32)
    grows = jnp.where(used, jnp.clip(cnt[gexp] - (gblk - blk_start[gexp]) * EXPERT_ROWS, 0, EXPERT_ROWS), 0)
    valid = (jnp.arange(n_pad) < n)[:, None]
    pos = jnp.where(valid, (blk_start * EXPERT_ROWS)[idx] + rank, 0).reshape(-1).astype(jnp.int32)
    w_flat = jnp.where(valid, wts, 0.0).reshape(-1)

    n_rows = n_groups * EXPERT_ROWS
    h3 = jnp.pad(h2, ((0, n_pad - n), (0, 0))).reshape(n_pad, d // LANES, LANES)
    xs3 = _scatter_rows(pos, h3, n_rows, n)
    ys = _experts(gblk.astype(jnp.int32), gexp, grows.astype(jnp.int32), xs3.reshape(n_rows, d), w1, b1, w2, b2)
    out3 = _combine(pos, w_flat, ys.reshape(n_rows, d // LANES, LANES), n_pad)
    return out3.reshape(n_pad, d)


def _rope_tables(pos):
    half = HEAD_DIM // 2
    inv_freq = 1.0 / (ROPE_THETA ** (jnp.arange(half, dtype=F32) / half))
    ang = pos.astype(F32)[:, None] * inv_freq[None, :]
    cos, sin = jnp.cos(ang), jnp.sin(ang)
    return jnp.concatenate([cos, cos], axis=1), jnp.concatenate([-sin, sin], axis=1)


def _in_proj(h, w_in, rope, d, *, tm, tn, precise=False):
    kvw = d // 4
    mm = functools.partial(_matmul, h, w_in, tm=tm, tn=tn, precise=precise)
    qa = mm(0, d, rope=rope)
    ka = mm(d, kvw, rope=rope)
    va = mm(d + kvw, kvw)
    qb = mm(d + 2 * kvw, d, rope=rope)
    kb = mm(2 * d + 2 * kvw, kvw, rope=rope)
    vb = mm(2 * d + 3 * kvw, kvw)
    gates = mm(2 * d + 4 * kvw, 2 * d)
    return qa, ka, va, qb, kb, vb, gates


def kernel(x_prompt, x_sample, c_prompt, c_sample, cache_moba_k, cache_moba_v, cache_diff_k, cache_diff_v, page_table, w_ada, b_ada, g_pre_mix, g_post_mix, g_pre_ffn, g_post_ffn, w_in, w_out, lambda_q1, lambda_k1, lambda_q2, lambda_k2, diff_subln, w_router, b_router, w_mlp1, b_mlp1, w_mlp2, b_mlp2):
    depth = w_in.shape[0]
    assert depth == 1 and x_prompt.shape[0] == 1 and x_sample.shape[1] == 1
    _, seq, d = x_prompt.shape
    nseq = x_sample.shape[0]
    n_pool = cache_moba_k.shape[1]
    past = page_table.shape[1] * PAGE_SIZE
    lam_init = 0.8 - 0.6 * math.exp(-0.3 * 0)
    row = lambda v: v[0][None, :]
    lam_params = [row(lambda_q1), row(lambda_k1), row(lambda_q2), row(lambda_k2)]
    subln = row(diff_subln)
    nh_a = d // HEAD_DIM
    kv_a = nh_a // MOBA_GROUP
    nh_b = d // (2 * HEAD_DIM)
    kv_b = nh_b // DIFF_GROUP

    c_all = jnp.concatenate([c_prompt, c_sample], axis=0)
    mod = _matmul(c_all, w_ada[0], 0, 6 * d, tm=c_all.shape[0], tn=512, bias=b_ada, silu_a=True, precise=True)
    mods_p = [mod[0:1, i * d:(i + 1) * d] for i in range(6)]
    mods_s = [mod[1:, i * d:(i + 1) * d] for i in range(6)]

    xp = x_prompt[0]
    shift_m, scale_m, gate_m, shift_f, scale_f, gate_f = mods_p
    h = _prenorm(xp, row(g_pre_mix), scale_m, shift_m, tm=256)
    rope_p = _rope_tables(jnp.arange(seq, dtype=jnp.int32))
    qa, ka, va, qb, kb, vb, gates = _in_proj(h, w_in[0], rope_p, d, tm=512, tn=512)
    o_a = _moba_prompt(qa, ka, va)
    o_b = _diff_prompt(qb, kb.astype(BF16), vb.astype(BF16), lam_params, subln, lam_init)
    z = _matmul(_merge(gates, o_a, o_b, tm=256), w_out[0], 0, d, tm=512, tn=512)
    x1_p, h2_p, lg_p = _post_mix(xp, z, gate_m, row(g_post_mix), row(g_pre_ffn), scale_f, shift_f,
                                 w_router[0], b_router, tm=256)
    gate_f_p = gate_f

    xs = x_sample[:, 0, :]
    shift_m, scale_m, gate_m, shift_f, scale_f, gate_f = mods_s
    h = _prenorm(xs, row(g_pre_mix), scale_m, shift_m, tm=nseq, dtype=F32)
    rope_s = _rope_tables(jnp.full((nseq,), past, jnp.int32))
    qa_s, ka_s, va_s, qb_s, kb_s, vb_s, gates_s = _in_proj(h, w_in[0], rope_s, d, tm=nseq, tn=512, precise=True)

    qf_a = qa_s.reshape(nseq, nh_a, HEAD_DIM)
    own_a = (jnp.arange(nh_a) // MOBA_GROUP)[:, None] == jnp.arange(kv_a)[None, :]
    qbd_a = jnp.where(own_a[None, :, :, None], qf_a[:, :, None, :], 0.0).reshape(nseq, nh_a, kv_a * HEAD_DIM)
    per_head = lambda t, width: jnp.repeat(t.reshape(nseq, -1, width), MOBA_GROUP, axis=1)
    o_a_s = _moba_decode(page_table, qbd_a, qf_a, per_head(ka_s, HEAD_DIM), per_head(va_s, HEAD_DIM),
                         cache_moba_k[0].reshape(n_pool, PAGE_SIZE * kv_a, HEAD_DIM),
                         cache_moba_v[0].reshape(n_pool, PAGE_SIZE * kv_a, HEAD_DIM)).reshape(nseq, d)

    qf_b = jnp.swapaxes(qb_s.reshape(nseq, nh_b, 2, HEAD_DIM), 1, 2).reshape(nseq, 2 * nh_b, HEAD_DIM)
    rr = jnp.arange(2 * nh_b)
    kcol = ((rr % nh_b) // DIFF_GROUP) * 2 + rr // nh_b
    own_b = kcol[:, None] == jnp.arange(2 * kv_b)[None, :]
    qbd_b = jnp.where(own_b[None, :, :, None], qf_b[:, :, None, :], 0.0).reshape(nseq, 2 * nh_b, 2 * kv_b * HEAD_DIM)
    knew_b = kb_s.reshape(nseq, 2 * kv_b, HEAD_DIM)[:, kcol, :]
    vnew_b = vb_s.reshape(nseq, kv_b, 2 * HEAD_DIM)[:, (rr % nh_b) // DIFF_GROUP, :]
    o_b_s = _diff_decode(page_table, qbd_b, qf_b, knew_b, vnew_b,
                         cache_diff_k[0].reshape(n_pool, PAGE_SIZE * 2 * kv_b, HEAD_DIM), cache_diff_v[0],
                         lam_params, subln, lam_init).reshape(nseq, d)

    z = _matmul(_merge(gates_s, o_a_s, o_b_s, tm=nseq, dtype=F32), w_out[0], 0, d, tm=nseq, tn=512, precise=True)
    x1_s, h2_s, lg_s = _post_mix(xs, z, gate_m, row(g_post_mix), row(g_pre_ffn), scale_f, shift_f,
                                 w_router[0], b_router, tm=nseq)

    moe = _moe(jnp.concatenate([h2_p, h2_s], axis=0), jnp.concatenate([lg_p, lg_s], axis=0),
               w_mlp1[0], b_mlp1[0], w_mlp2[0], b_mlp2[0])
    y_p = _post_ffn(x1_p, moe[:seq], gate_f_p, row(g_post_ffn), tm=256)
    y_s = _post_ffn(x1_s, moe[seq:seq + nseq], gate_f, row(g_post_ffn), tm=nseq)

    return (y_p[None], y_s[:, None, :],
            ka.reshape(1, 1, seq, kv_a, HEAD_DIM), va.reshape(1, 1, seq, kv_a, HEAD_DIM),
            kb.reshape(1, 1, seq, kv_b, 2, HEAD_DIM), vb.reshape(1, 1, seq, kv_b, 2 * HEAD_DIM),
            ka_s.reshape(1, nseq, 1, kv_a, HEAD_DIM), va_s.reshape(1, nseq, 1, kv_a, HEAD_DIM),
            kb_s.reshape(1, nseq, 1, kv_b, 2, HEAD_DIM), vb_s.reshape(1, nseq, 1, kv_b, 2 * HEAD_DIM))
```
